```python
import math
import jax, jax.numpy as jnp
from jax import lax
import numpy as np

D_MODEL = 2048
BATCH = 8
SEQ = 2048
DEPTH = 2

CTX_LEN = 256
GRID_W = 64
N_MIXERS = 2
N_LAYERS_A = (DEPTH + 1) // 2
N_LAYERS_B = DEPTH // 2
FFN_HIDDEN = -(-8 * D_MODEL // (3 * 256)) * 256
NORM_EPS = 1e-6
ROPE_BASE = 10000.0
Q_BLOCK = 128

DA_HEAD_DIM = 128
DA_HEADS = D_MODEL // (2 * DA_HEAD_DIM)
DA_V_DIM = 2 * DA_HEAD_DIM
DA_QK_WIDTH = DA_HEADS * 2 * DA_HEAD_DIM
DA_V_WIDTH = DA_HEADS * DA_V_DIM
DA_SCALE = DA_HEAD_DIM ** -0.5

MLA_HEADS = D_MODEL // 128
MLA_NOPE = 128
MLA_ROPE = 64
MLA_V = 128
MLA_Q_LORA = 512
MLA_KV_LORA = 512
MLA_IN_WIDTH = MLA_Q_LORA + MLA_KV_LORA + MLA_ROPE
MLA_SCALE = (MLA_NOPE + MLA_ROPE) ** -0.5

kernel_name = 'hybrid_diffattn_mla_dit_prefix'


def rms_norm(x, g):
    xf = x.astype(jnp.float32)
    y = xf * lax.rsqrt(jnp.mean(xf * xf, axis=-1, keepdims=True) + NORM_EPS)
    return (y * g.astype(jnp.float32)).astype(x.dtype)


def modulate(h, shift, scale):
    return h * (1 + scale) + shift


def swiglu(h, w_in, w_out):
    g, u = jnp.split(h @ w_in, 2, axis=-1)
    return (jax.nn.silu(g) * u) @ w_out


def axial_rope(rows, rot_dim):
    row = jnp.repeat(jnp.arange(rows, dtype=jnp.float32), GRID_W)
    col = jnp.tile(jnp.arange(GRID_W, dtype=jnp.float32), rows)
    n_freq = rot_dim // 4
    inv_freq = ROPE_BASE ** (-jnp.arange(n_freq, dtype=jnp.float32) / n_freq)
    ang = jnp.concatenate([row[:, None] * inv_freq, col[:, None] * inv_freq], axis=-1)
    return jnp.cos(ang), jnp.sin(ang)


def apply_rope(x, cos, sin):
    x1, x2 = jnp.split(x, 2, axis=-1)
    return jnp.concatenate([x1 * cos - x2 * sin, x1 * sin + x2 * cos], axis=-1).astype(x.dtype)


def sweep_query_blocks(fn, *q_arrays):
    B, S = q_arrays[0].shape[:2]
    nb = S // Q_BLOCK
    blocks = tuple(jnp.moveaxis(a.reshape(B, nb, Q_BLOCK, *a.shape[2:]), 1, 0) for a in q_arrays)
    out = lax.map(lambda qs: fn(*qs), blocks)
    out = jnp.moveaxis(out, 0, 1)
    return out.reshape(B, S, *out.shape[3:])


def diff_attend(q, k, v, lam):
    s = jnp.einsum('bqhmd,bkhmd->bhmqk', q, k).astype(jnp.float32) * DA_SCALE
    p = jax.nn.softmax(s, axis=-1)
    a = p[:, :, 0] - lam * p[:, :, 1]
    return jnp.einsum('bhqk,bkhe->bqhe', a.astype(v.dtype), v)


def diff_attention(h_lat, h_ctx, w_qkv, lam_vecs, subln_g, w_o, lam_init, cos, sin, ctx_out):
    def project(h):
        B, T, _ = h.shape
        q, k, v = jnp.split(h @ w_qkv, [DA_QK_WIDTH, 2 * DA_QK_WIDTH], axis=-1)
        return (q.reshape(B, T, DA_HEADS, 2, DA_HEAD_DIM),
                k.reshape(B, T, DA_HEADS, 2, DA_HEAD_DIM),
                v.reshape(B, T, DA_HEADS, DA_V_DIM))

    def finish(o):
        B, T = o.shape[:2]
        o = rms_norm(o, subln_g) * (1.0 - lam_init)
        return o.reshape(B, T, DA_V_WIDTH) @ w_o

    lv = lam_vecs.astype(jnp.float32)
    lam = jnp.exp(jnp.sum(lv[0] * lv[1])) - jnp.exp(jnp.sum(lv[2] * lv[3])) + lam_init

    q_c, k_c, v_c = project(h_ctx)
    q_l, k_l, v_l = project(h_lat)
    cs, sn = cos[None, :, None, None, :], sin[None, :, None, None, :]
    q_l = apply_rope(q_l, cs, sn)
    k_l = apply_rope(k_l, cs, sn)
    k_all = jnp.concatenate([k_c, k_l], axis=1)
    v_all = jnp.concatenate([v_c, v_l], axis=1)

    out_lat = finish(sweep_query_blocks(lambda qb: diff_attend(qb, k_all, v_all, lam), q_l))
    out_ctx = finish(diff_attend(q_c, k_c, v_c, lam)) if ctx_out else None
    return out_lat, out_ctx


def mla_project(h, w_in, q_norm_g, w_q_up, kv_norm_g, w_kv_up, cos=None, sin=None):
    B, T, _ = h.shape
    c_q, c_kv, k_rope = jnp.split(h @ w_in, [MLA_Q_LORA, MLA_Q_LORA + MLA_KV_LORA], axis=-1)
    q = (rms_norm(c_q, q_norm_g) @ w_q_up).reshape(B, T, MLA_HEADS, MLA_NOPE + MLA_ROPE)
    q_nope, q_rope = jnp.split(q, [MLA_NOPE], axis=-1)
    kv = (rms_norm(c_kv, kv_norm_g) @ w_kv_up).reshape(B, T, MLA_HEADS, MLA_NOPE + MLA_V)
    k_nope, v = jnp.split(kv, [MLA_NOPE], axis=-1)
    if cos is not None:
        q_rope = apply_rope(q_rope, cos[None, :, None, :], sin[None, :, None, :])
        k_rope = apply_rope(k_rope, cos[None], sin[None])
    return q_nope, q_rope, k_nope, k_rope, v


def mla_attend(q_nope, q_rope, k_nope, k_rope, v):
    s = (jnp.einsum('bqhd,bkhd->bhqk', q_nope, k_nope)
         + jnp.einsum('bqhr,bkr->bhqk', q_rope, k_rope)).astype(jnp.float32) * MLA_SCALE
    p = jax.nn.softmax(s, axis=-1)
    return jnp.einsum('bhqk,bkhd->bqhd', p.astype(v.dtype), v)


def mla_attention(h_lat, h_ctx, w_in, q_norm_g, w_q_up, kv_norm_g, w_kv_up, w_o, cos, sin, ctx_out):
    qn_c, qr_c, kn_c, kr_c, v_c = mla_project(h_ctx, w_in, q_norm_g, w_q_up, kv_norm_g, w_kv_up)
    qn_l, qr_l, kn_l, kr_l, v_l = mla_project(h_lat, w_in, q_norm_g, w_q_up, kv_norm_g, w_kv_up, cos, sin)
    kn_all = jnp.concatenate([kn_c, kn_l], axis=1)
    kr_all = jnp.concatenate([kr_c, kr_l], axis=1)
    v_all = jnp.concatenate([v_c, v_l], axis=1)

    def finish(o):
        B, T = o.shape[:2]
        return o.reshape(B, T, MLA_HEADS * MLA_V) @ w_o

    out_lat = finish(sweep_query_blocks(
        lambda qn, qr: mla_attend(qn, qr, kn_all, kr_all, v_all), qn_l, qr_l))
    out_ctx = finish(mla_attend(qn_c, qr_c, kn_c, kr_c, v_c)) if ctx_out else None
    return out_lat, out_ctx


def _normal(key, shape, scale):
    return jax.random.normal(key, shape, jnp.float32) * scale


def setup_inputs(seed: int = 0) -> dict:
    key = jax.random.key(seed)
    ks = jax.random.split(key, 24)
    D = D_MODEL
    return {
        'x': _normal(ks[0], (BATCH, SEQ, D), 1.0),
        'c': _normal(ks[1], (BATCH, D), 1.0),
        'ctx': _normal(ks[2], (BATCH, CTX_LEN, D), 1.0),
        'c_ctx': _normal(ks[3], (D,), 1.0),
        'ada_w': _normal(ks[4], (DEPTH, D, 6 * D), 0.5 * D ** -0.5),
        'ada_b': _normal(ks[5], (DEPTH, 6 * D), 0.02),
        'norm_mix_g': 1.0 + _normal(ks[6], (DEPTH, D), 0.02),
        'norm_ffn_g': 1.0 + _normal(ks[7], (DEPTH, D), 0.02),
        'ffn_w_in': _normal(ks[8], (DEPTH, D, 2 * FFN_HIDDEN), D ** -0.5),
        'ffn_w_out': _normal(ks[9], (DEPTH, FFN_HIDDEN, D), FFN_HIDDEN ** -0.5),
        'da_w_qkv': _normal(ks[10], (N_LAYERS_A, D, 2 * DA_QK_WIDTH + DA_V_WIDTH), D ** -0.5),
        'da_lambda': _normal(ks[11], (N_LAYERS_A, 4, DA_HEAD_DIM), 0.1),
        'da_subln_g': 1.0 + _normal(ks[12], (N_LAYERS_A, DA_V_DIM), 0.02),
        'da_w_o': _normal(ks[13], (N_LAYERS_A, DA_V_WIDTH, D), DA_V_WIDTH ** -0.5),
        'mla_w_in': _normal(ks[14], (N_LAYERS_B, D, MLA_IN_WIDTH), D ** -0.5),
        'mla_q_norm_g': 1.0 + _normal(ks[15], (N_LAYERS_B, MLA_Q_LORA), 0.02),
        'mla_w_q_up': _normal(ks[16], (N_LAYERS_B, MLA_Q_LORA, MLA_HEADS * (MLA_NOPE + MLA_ROPE)), MLA_Q_LORA ** -0.5),
        'mla_kv_norm_g': 1.0 + _normal(ks[17], (N_LAYERS_B, MLA_KV_LORA), 0.02),
        'mla_w_kv_up': _normal(ks[18], (N_LAYERS_B, MLA_KV_LORA, MLA_HEADS * (MLA_NOPE + MLA_V)), MLA_KV_LORA ** -0.5),
        'mla_w_o': _normal(ks[19], (N_LAYERS_B, MLA_HEADS * MLA_V, D), (MLA_HEADS * MLA_V) ** -0.5),
        'final_norm_g': 1.0 + _normal(ks[20], (D,), 0.02),
    }


def reference(x, c, ctx, c_ctx, ada_w, ada_b, norm_mix_g, norm_ffn_g, ffn_w_in, ffn_w_out,
              da_w_qkv, da_lambda, da_subln_g, da_w_o,
              mla_w_in, mla_q_norm_g, mla_w_q_up, mla_kv_norm_g, mla_w_kv_up, mla_w_o,
              final_norm_g):
    ROWS = x.shape[1] // GRID_W
    da_cos, da_sin = axial_rope(ROWS, DA_HEAD_DIM)
    mla_cos, mla_sin = axial_rope(ROWS, MLA_ROPE)
    silu_c = jax.nn.silu(c)
    silu_cc = jax.nn.silu(c_ctx)
    xc = ctx
    for i in range(DEPTH):
        ctx_out = i < DEPTH - 1
        j = i // N_MIXERS
        mod = jnp.split(silu_c @ ada_w[i] + ada_b[i], 6, axis=-1)
        mod = [m[:, None, :] for m in mod]
        mod_c = jnp.split(silu_cc @ ada_w[i] + ada_b[i], 6, axis=-1)
        h_lat = modulate(rms_norm(x, norm_mix_g[i]), mod[0], mod[1])
        h_ctx = modulate(rms_norm(xc, norm_mix_g[i]), mod_c[0], mod_c[1])
        if i % N_MIXERS == 0:
            lam_init = 0.8 - 0.6 * math.exp(-0.3 * i)
            a_lat, a_ctx = diff_attention(h_lat, h_ctx, da_w_qkv[j], da_lambda[j], da_subln_g[j],
                                          da_w_o[j], lam_init, da_cos, da_sin, ctx_out)
        else:
            a_lat, a_ctx = mla_attention(h_lat, h_ctx, mla_w_in[j], mla_q_norm_g[j], mla_w_q_up[j],
                                         mla_kv_norm_g[j], mla_w_kv_up[j], mla_w_o[j],
                                         mla_cos, mla_sin, ctx_out)
        x = x + mod[2] * a_lat
        x = x + mod[5] * swiglu(modulate(rms_norm(x, norm_ffn_g[i]), mod[3], mod[4]), ffn_w_in[i], ffn_w_out[i])
        if ctx_out:
            xc = xc + mod_c[2] * a_ctx
            xc = xc + mod_c[5] * swiglu(modulate(rms_norm(xc, norm_ffn_g[i]), mod_c[3], mod_c[4]),
                                        ffn_w_in[i], ffn_w_out[i])
    return rms_norm(x, final_norm_g)
```

```python
import functools
import math

import jax
import jax.numpy as jnp
from jax import lax
from jax.experimental import pallas as pl
from jax.experimental.pallas import tpu as pltpu

F32 = jnp.float32
BF16 = jnp.bfloat16

GRID_W = 64
NORM_EPS = 1e-6
ROPE_BASE = 10000.0
DA_HEAD_DIM = 128
DA_V_DIM = 2 * DA_HEAD_DIM
DA_SCALE = DA_HEAD_DIM ** -0.5
MLA_NOPE = 128
MLA_ROPE = 64
MLA_V = 128
MLA_Q_LORA = 512
MLA_KV_LORA = 512
MLA_SCALE = (MLA_NOPE + MLA_ROPE) ** -0.5
MLA_QK_PAD = 256

LANES = 128
V7X_VMEM_BYTES = 64 * 1024 * 1024
VMEM_LIMIT = V7X_VMEM_BYTES - 8 * 1024 * 1024

MOD_ROWS = 16


def _params(*sem):
    return pltpu.CompilerParams(dimension_semantics=sem, vmem_limit_bytes=VMEM_LIMIT)


def _rms(x, g):
    ms = jnp.mean(x * x, axis=-1, keepdims=True)
    return x * lax.rsqrt(ms + NORM_EPS) * g


def _rot_half(x):
    return pltpu.roll(x, LANES // 2, 1)


def _mod_kernel(c_ref, w_ref, b_ref, o_ref):
    c = c_ref[...]
    s = c * jax.nn.sigmoid(c)
    o_ref[0] = jnp.dot(s.astype(BF16), w_ref[0].astype(BF16), preferred_element_type=F32) + b_ref[0]


def _mods(cond, ada_w, ada_b, tn=1024):
    depth, d, n = ada_w.shape
    return pl.pallas_call(
        _mod_kernel,
        grid=(depth, n // tn),
        in_specs=[pl.BlockSpec((MOD_ROWS, d), lambda l, j: (0, 0)),
                  pl.BlockSpec((1, d, tn), lambda l, j: (l, 0, j)),
                  pl.BlockSpec((1, 1, tn), lambda l, j: (l, 0, j))],
        out_specs=pl.BlockSpec((1, MOD_ROWS, tn), lambda l, j: (l, 0, j)),
        out_shape=jax.ShapeDtypeStruct((depth, MOD_ROWS, n), F32),
        compiler_params=_params("arbitrary", "arbitrary"),
        name="adaln_mod",
    )(cond, ada_w, ada_b.reshape(depth, 1, n))


def _norm_mod_kernel(x_ref, g_ref, shift_ref, scale_ref, o_ref):
    y = _rms(x_ref[...], g_ref[...])
    o_ref[...] = (y * (1 + scale_ref[...]) + shift_ref[...]).astype(BF16)


def _group_spec(d, tiles_per_group):
    return pl.BlockSpec((None, 1, d), lambda i, *_: (i // tiles_per_group, 0, 0))


def _norm_mod(x, g, shift, scale, tm=512):
    t, d = x.shape
    tpg = t // shift.shape[0] // tm
    return pl.pallas_call(
        _norm_mod_kernel,
        grid=(t // tm,),
        in_specs=[pl.BlockSpec((tm, d), lambda i: (i, 0)),
                  pl.BlockSpec((1, d), lambda i: (0, 0)),
                  _group_spec(d, tpg), _group_spec(d, tpg)],
        out_specs=pl.BlockSpec((tm, d), lambda i: (i, 0)),
        out_shape=jax.ShapeDtypeStruct((t, d), BF16),
        compiler_params=_params("arbitrary"),
        name="norm_mod",
    )(x, g.reshape(1, d), shift, scale)


def _qkv_rope_kernel(a_ref, w_ref, cos_ref, sin_ref, o_ref):
    acc = jnp.dot(a_ref[...], w_ref[...], preferred_element_type=F32)
    cos = cos_ref[...]
    sin = sin_ref[...]
    for g in range(acc.shape[1] // LANES):
        sl = slice(g * LANES, (g + 1) * LANES)
        xg = acc[:, sl]
        o_ref[:, sl] = (xg * cos + _rot_half(xg) * sin).astype(BF16)


def _qkv_rope(a, w, cos_tab, sin_tab, tm=1024, tn=1024):
    t, k = a.shape
    n = w.shape[1]
    tiles_per_kind = n // 3 // tn
    row_tiles = cos_tab.shape[1] // tm
    tab_spec = pl.BlockSpec((None, tm, LANES), lambda i, j: (j // tiles_per_kind, i % row_tiles, 0))
    return pl.pallas_call(
        _qkv_rope_kernel,
        grid=(t // tm, n // tn),
        in_specs=[pl.BlockSpec((tm, k), lambda i, j: (i, 0)),
                  pl.BlockSpec((k, tn), lambda i, j: (0, j)),
                  tab_spec, tab_spec],
        out_specs=pl.BlockSpec((tm, tn), lambda i, j: (i, j)),
        out_shape=jax.ShapeDtypeStruct((t, n), BF16),
        compiler_params=_params("arbitrary", "arbitrary"),
        name="qkv_rope",
    )(a, w, cos_tab, sin_tab)


def _swiglu_kernel(a_ref, wg_ref, wu_ref, o_ref):
    a = a_ref[...]
    g = jnp.dot(a, wg_ref[...], preferred_element_type=F32)
    u = jnp.dot(a, wu_ref[...], preferred_element_type=F32)
    o_ref[...] = (g * jax.nn.sigmoid(g) * u).astype(BF16)


def _swiglu_in(a, w_in, tm=1024, tn=512):
    t, k = a.shape
    hidden = w_in.shape[1] // 2
    nj = hidden // tn
    return pl.pallas_call(
        _swiglu_kernel,
        grid=(t // tm, nj),
        in_specs=[pl.BlockSpec((tm, k), lambda i, j: (i, 0)),
                  pl.BlockSpec((k, tn), lambda i, j: (0, j)),
                  pl.BlockSpec((k, tn), lambda i, j: (0, j + nj))],
        out_specs=pl.BlockSpec((tm, tn), lambda i, j: (i, j)),
        out_shape=jax.ShapeDtypeStruct((t, hidden), BF16),
        compiler_params=_params("arbitrary", "arbitrary"),
        name="swiglu_in",
    )(a, w_in, w_in)


def _res_kernel(out_x, next_norm, a_ref, w_ref, x_ref, gate_ref, *refs):
    y = jnp.dot(a_ref[...], w_ref[...], preferred_element_type=F32)
    xn = x_ref[...] + gate_ref[...] * y
    refs = list(refs)
    if next_norm == "mod":
        g_ref, shift_ref, scale_ref = refs[:3]
        outs = refs[3:]
    else:
        g_ref = refs[0]
        outs = refs[1:]
    if out_x:
        outs.pop(0)[...] = xn
    h = _rms(xn, g_ref[...])
    if next_norm == "mod":
        outs[0][...] = (h * (1 + scale_ref[...]) + shift_ref[...]).astype(BF16)
    else:
        outs[0][...] = h


def _res(a, w, x, gate, g, shift=None, scale=None, *, out_x=True, tm):
    t, k = a.shape
    d = w.shape[1]
    next_norm = "final" if shift is None else "mod"
    tpg = t // gate.shape[0] // tm
    row = pl.BlockSpec((tm, d), lambda i: (i, 0))
    in_specs = [pl.BlockSpec((tm, k), lambda i: (i, 0)),
                pl.BlockSpec((k, d), lambda i: (0, 0), pipeline_mode=pl.Buffered(1)),
                row, _group_spec(d, tpg), pl.BlockSpec((1, d), lambda i: (0, 0))]
    args = [a, w, x, gate, g.reshape(1, d)]
    if next_norm == "mod":
        in_specs += [_group_spec(d, tpg), _group_spec(d, tpg)]
        args += [shift, scale]
    out_specs, out_shape = [], []
    if out_x:
        out_specs.append(row)
        out_shape.append(jax.ShapeDtypeStruct((t, d), F32))
    out_specs.append(row)
    out_shape.append(jax.ShapeDtypeStruct((t, d), BF16 if next_norm == "mod" else F32))
    return pl.pallas_call(
        functools.partial(_res_kernel, out_x, next_norm),
        grid=(t // tm,),
        in_specs=in_specs, out_specs=out_specs, out_shape=out_shape,
        compiler_params=_params("arbitrary"),
        name="proj_residual",
    )(*args)


def _nt_dot(q, k):
    return lax.dot_general(q, k, (((1,), (1,)), ((), ())), preferred_element_type=F32)


def _softmax_parts(scores):
    mx = functools.reduce(jnp.maximum, [jnp.max(s, axis=1, keepdims=True) for s in scores])
    e = [jnp.exp(s - mx) for s in scores]
    tot = functools.reduce(jnp.add, [jnp.sum(x, axis=1, keepdims=True) for x in e])
    return e, 1.0 / tot


def _da_attn_kernel(nseg, lam_init, lamv_ref, g_ref, q_ref, *refs):
    k_refs, v_refs, o_ref = refs[:nseg], refs[nseg:2 * nseg], refs[2 * nseg]
    lv = lamv_ref[...]
    lam = (jnp.exp(jnp.sum(lv[0:1] * lv[1:2], axis=1, keepdims=True))
           - jnp.exp(jnp.sum(lv[2:3] * lv[3:4], axis=1, keepdims=True)) + lam_init)
    a = None
    for m in range(2):
        sl = slice(m * DA_HEAD_DIM, (m + 1) * DA_HEAD_DIM)
        q = q_ref[:, sl]
        e, r = _softmax_parts([_nt_dot(q, k[:, sl]) for k in k_refs])
        if m == 0:
            a = [x * r for x in e]
        else:
            rl = lam * r
            a = [a0 - x * rl for a0, x in zip(a, e)]
    o = functools.reduce(jnp.add, [jnp.dot(p.astype(BF16), v[...], preferred_element_type=F32)
                                   for p, v in zip(a, v_refs)])
    o_ref[...] = (_rms(o, g_ref[...]) * (1.0 - lam_init)).astype(BF16)


def _da_attention(q_src, kv_srcs, lam_vecs, subln_g, lam_init, heads, tq):
    b, sq, _ = q_src.shape
    hd = DA_V_DIM
    in_specs = [pl.BlockSpec((4, DA_HEAD_DIM), lambda bi, h, qi: (0, 0)),
                pl.BlockSpec((1, hd), lambda bi, h, qi: (0, 0)),
                pl.BlockSpec((None, tq, hd), lambda bi, h, qi: (bi, qi, h))]
    args = [lam_vecs, subln_g.reshape(1, hd), q_src]
    for kind in (1, 2):
        for src in kv_srcs:
            in_specs.append(pl.BlockSpec((None, src.shape[1], hd),
                                         lambda bi, h, qi, kind=kind: (bi, 0, kind * heads + h)))
            args.append(src)
    return pl.pallas_call(
        functools.partial(_da_attn_kernel, len(kv_srcs), lam_init),
        grid=(b, heads, sq // tq),
        in_specs=in_specs,
        out_specs=pl.BlockSpec((None, tq, hd), lambda bi, h, qi: (bi, qi, h)),
        out_shape=jax.ShapeDtypeStruct((b, sq, heads * hd), BF16),
        compiler_params=_params("arbitrary", "arbitrary", "arbitrary"),
        name="diff_attention",
    )(*args)


def _mla_attn_kernel(nseg, q_ref, *refs):
    k_refs, v_refs, o_ref = refs[:nseg], refs[nseg:2 * nseg], refs[2 * nseg]
    q = q_ref[...]
    e, r = _softmax_parts([_nt_dot(q, k[...]) for k in k_refs])
    o = functools.reduce(jnp.add, [jnp.dot(p.astype(BF16), v[...], preferred_element_type=F32)
                                   for p, v in zip(e, v_refs)])
    o_ref[...] = (o * r).astype(BF16)


def _mla_attention(q, ks, vs, heads, tq):
    b, sq, _ = q.shape
    in_specs = [pl.BlockSpec((None, tq, MLA_QK_PAD), lambda bi, h, qi: (bi, qi, h))]
    in_specs += [pl.BlockSpec((None, k.shape[1], MLA_QK_PAD), lambda bi, h, qi: (bi, 0, h)) for k in ks]
    in_specs += [pl.BlockSpec((None, v.shape[1], MLA_V), lambda bi, h, qi: (bi, 0, h)) for v in vs]
    return pl.pallas_call(
        functools.partial(_mla_attn_kernel, len(ks)),
        grid=(b, heads, sq // tq),
        in_specs=in_specs,
        out_specs=pl.BlockSpec((None, tq, MLA_V), lambda bi, h, qi: (bi, qi, h)),
        out_shape=jax.ShapeDtypeStruct((b, sq, heads * MLA_V), BF16),
        compiler_params=_params("arbitrary", "arbitrary", "arbitrary"),
        name="mla_attention",
    )(q, *ks, *vs)


def _mla_in_kernel(a_ref, w_ref, qg_ref, kvg_ref, cos_ref, sin_ref, cq_ref, ckv_ref, kr_ref):
    acc = jnp.dot(a_ref[...], w_ref[...], preferred_element_type=F32)
    cq_ref[...] = _rms(acc[:, :MLA_Q_LORA], qg_ref[...]).astype(BF16)
    ckv_ref[...] = _rms(acc[:, MLA_Q_LORA:MLA_Q_LORA + MLA_KV_LORA], kvg_ref[...]).astype(BF16)
    kr = acc[:, MLA_Q_LORA + MLA_KV_LORA:]
    kr_ref[...] = (kr * cos_ref[...] + _rot_half(kr) * sin_ref[...]).astype(BF16)


def _mla_in(a, w, q_g, kv_g, cos_tab, sin_tab, tm=512):
    t, k = a.shape
    n = w.shape[1]
    row_tiles = cos_tab.shape[0] // tm
    tab = pl.BlockSpec((tm, LANES), lambda i: (i % row_tiles, 0))
    outs = [(MLA_Q_LORA, BF16), (MLA_KV_LORA, BF16), (LANES, BF16)]
    return pl.pallas_call(
        _mla_in_kernel,
        grid=(t // tm,),
        in_specs=[pl.BlockSpec((tm, k), lambda i: (i, 0)),
                  pl.BlockSpec((k, n), lambda i: (0, 0), pipeline_mode=pl.Buffered(1)),
                  pl.BlockSpec((1, MLA_Q_LORA), lambda i: (0, 0)),
                  pl.BlockSpec((1, MLA_KV_LORA), lambda i: (0, 0)),
                  tab, tab],
        out_specs=[pl.BlockSpec((tm, w_), lambda i: (i, 0)) for w_, _ in outs],
        out_shape=[jax.ShapeDtypeStruct((t, w_), dt) for w_, dt in outs],
        compiler_params=_params("arbitrary"),
        name="mla_down",
    )(a, w, q_g.reshape(1, -1), kv_g.reshape(1, -1), cos_tab, sin_tab)


def _q_up_kernel(a_ref, w_ref, cos_ref, sin_ref, o_ref):
    acc = jnp.dot(a_ref[...], w_ref[...], preferred_element_type=F32)
    cos = cos_ref[...]
    sin = sin_ref[...]
    for h in range(acc.shape[1] // MLA_QK_PAD):
        base = h * MLA_QK_PAD
        o_ref[:, base:base + MLA_NOPE] = (acc[:, base:base + MLA_NOPE] * MLA_SCALE).astype(BF16)
        xr = acc[:, base + MLA_NOPE:base + MLA_QK_PAD]
        o_ref[:, base + MLA_NOPE:base + MLA_QK_PAD] = (xr * cos + _rot_half(xr) * sin).astype(BF16)


def _q_up(a, w, cos_tab, sin_tab, tm=1024, tn=1024):
    t, k = a.shape
    n = w.shape[1]
    row_tiles = cos_tab.shape[0] // tm
    tab = pl.BlockSpec((tm, LANES), lambda i, j: (i % row_tiles, 0))
    return pl.pallas_call(
        _q_up_kernel,
        grid=(t // tm, n // tn),
        in_specs=[pl.BlockSpec((tm, k), lambda i, j: (i, 0)),
                  pl.BlockSpec((k, tn), lambda i, j: (0, j)),
                  tab, tab],
        out_specs=pl.BlockSpec((tm, tn), lambda i, j: (i, j)),
        out_shape=jax.ShapeDtypeStruct((t, n), BF16),
        compiler_params=_params("arbitrary", "arbitrary"),
        name="mla_q_up",
    )(a, w, cos_tab, sin_tab)


def _kv_up_kernel(a_ref, kr_ref, wk_ref, wv_ref, k_ref, v_ref):
    a = a_ref[...]
    kn = jnp.dot(a, wk_ref[...], preferred_element_type=F32).astype(BF16)
    v_ref[...] = jnp.dot(a, wv_ref[...], preferred_element_type=F32).astype(BF16)
    kr = kr_ref[...]
    for h in range(kn.shape[1] // MLA_NOPE):
        base = h * MLA_QK_PAD
        k_ref[:, base:base + MLA_NOPE] = kn[:, h * MLA_NOPE:(h + 1) * MLA_NOPE]
        k_ref[:, base + MLA_NOPE:base + MLA_QK_PAD] = kr


def _kv_up(a, kr, wk, wv, tm=512):
    t, k = a.shape
    nk, nv = wk.shape[1], wv.shape[1]
    heads = nk // MLA_NOPE
    return pl.pallas_call(
        _kv_up_kernel,
        grid=(t // tm,),
        in_specs=[pl.BlockSpec((tm, k), lambda i: (i, 0)),
                  pl.BlockSpec((tm, LANES), lambda i: (i, 0)),
                  pl.BlockSpec((k, nk), lambda i: (0, 0), pipeline_mode=pl.Buffered(1)),
                  pl.BlockSpec((k, nv), lambda i: (0, 0), pipeline_mode=pl.Buffered(1))],
        out_specs=[pl.BlockSpec((tm, heads * MLA_QK_PAD), lambda i: (i, 0)),
                   pl.BlockSpec((tm, nv), lambda i: (i, 0))],
        out_shape=[jax.ShapeDtypeStruct((t, heads * MLA_QK_PAD), BF16),
                   jax.ShapeDtypeStruct((t, nv), BF16)],
        compiler_params=_params("arbitrary"),
        name="mla_kv_up",
    )(a, kr, wk, wv)


def _axial_angles(rows, rot_dim):
    row = jnp.repeat(jnp.arange(rows, dtype=F32), GRID_W)
    col = jnp.tile(jnp.arange(GRID_W, dtype=F32), rows)
    n_freq = rot_dim // 4
    inv_freq = ROPE_BASE ** (-jnp.arange(n_freq, dtype=F32) / n_freq)
    ang = jnp.concatenate([row[:, None] * inv_freq, col[:, None] * inv_freq], axis=-1)
    return jnp.cos(ang), jnp.sin(ang)


def _da_tables(seq, ctx_rows):
    cos, sin = _axial_angles(seq // GRID_W, DA_HEAD_DIM)
    cos = jnp.concatenate([cos, cos], axis=-1)
    sin = jnp.concatenate([-sin, sin], axis=-1)
    one, zero = jnp.ones_like(cos), jnp.zeros_like(cos)
    lat = (jnp.stack([cos * DA_SCALE, cos, one]), jnp.stack([sin * DA_SCALE, sin, zero]))
    one_c, zero_c = one[:ctx_rows], zero[:ctx_rows]
    ctx = (jnp.stack([one_c * DA_SCALE, one_c, one_c]), jnp.stack([zero_c, zero_c, zero_c]))
    return lat, ctx


def _spread(x1, x2):
    z = jnp.zeros_like(x1)
    return jnp.concatenate([x1, z, x2, z], axis=-1)


def _mla_tables(seq, ctx_rows):
    cos, sin = _axial_angles(seq // GRID_W, MLA_ROPE)
    cos_l, sin_l = _spread(cos, cos), _spread(-sin, sin)
    cos_c, sin_c = jnp.ones_like(cos_l[:ctx_rows]), jnp.zeros_like(cos_l[:ctx_rows])
    return (cos_l, sin_l), (cos_c, sin_c)


def _mla_weight_layouts(w_in, w_q_up, w_kv_up):
    half = MLA_ROPE // 2
    lora = MLA_Q_LORA + MLA_KV_LORA
    w_in_p = jnp.concatenate([w_in[:, :lora], _spread(w_in[:, lora:lora + half], w_in[:, lora + half:])], axis=1)
    wq = w_q_up.reshape(MLA_Q_LORA, -1, MLA_NOPE + MLA_ROPE)
    wq_p = jnp.concatenate([wq[..., :MLA_NOPE],
                            _spread(wq[..., MLA_NOPE:MLA_NOPE + half], wq[..., MLA_NOPE + half:])], axis=-1)
    wkv = w_kv_up.reshape(MLA_KV_LORA, -1, MLA_NOPE + MLA_V)
    wk = wkv[..., :MLA_NOPE].reshape(MLA_KV_LORA, -1)
    wv = wkv[..., MLA_NOPE:].reshape(MLA_KV_LORA, -1)
    return w_in_p, wq_p.reshape(MLA_Q_LORA, -1), wk, wv


def kernel(x, c, ctx, c_ctx, ada_w, ada_b, norm_mix_g, norm_ffn_g, ffn_w_in, ffn_w_out, da_w_qkv, da_lambda, da_subln_g, da_w_o, mla_w_in, mla_q_norm_g, mla_w_q_up, mla_kv_norm_g, mla_w_kv_up, mla_w_o, final_norm_g):
    b, s, d = x.shape
    cl = ctx.shape[1]
    assert ada_w.shape[0] == 2 and b + 1 <= MOD_ROWS
    da_heads = d // DA_V_DIM
    mla_heads = d // MLA_V

    cond = jnp.concatenate([c, c_ctx[None], jnp.zeros((MOD_ROWS - b - 1, d), F32)], axis=0)
    mods = _mods(cond, ada_w, ada_b)

    def lat_mod(i, k):
        return mods[i, :b, k * d:(k + 1) * d].reshape(b, 1, d)

    def ctx_mod(i, k):
        return mods[i, b:b + 1, k * d:(k + 1) * d].reshape(1, 1, d)

    bf = lambda w: w.astype(BF16)
    w_qkv, w_o0 = bf(da_w_qkv[0]), bf(da_w_o[0])
    w_ffn_in, w_ffn_out = bf(ffn_w_in), bf(ffn_w_out)
    w_in_p, wq_p, wk, wv = (bf(w) for w in _mla_weight_layouts(mla_w_in[0], mla_w_q_up[0], mla_w_kv_up[0]))
    w_o1 = bf(mla_w_o[0])
    da_lat_tab, da_ctx_tab = _da_tables(s, 1024)
    mla_lat_tab, mla_ctx_tab = _mla_tables(s, 512)

    xl = x.reshape(b * s, d)
    xc = ctx.reshape(b * cl, d)

    lam_init = 0.8 - 0.6 * math.exp(-0.3 * 0)
    h_l = _norm_mod(xl, norm_mix_g[0], lat_mod(0, 0), lat_mod(0, 1))
    h_c = _norm_mod(xc, norm_mix_g[0], ctx_mod(0, 0), ctx_mod(0, 1))
    qkv_l = _qkv_rope(h_l, w_qkv, *da_lat_tab).reshape(b, s, -1)
    qkv_c = _qkv_rope(h_c, w_qkv, *da_ctx_tab).reshape(b, cl, -1)
    a_l = _da_attention(qkv_l, [qkv_c, qkv_l], da_lambda[0], da_subln_g[0], lam_init, da_heads, tq=256)
    a_c = _da_attention(qkv_c, [qkv_c], da_lambda[0], da_subln_g[0], lam_init, da_heads, tq=cl)
    xl, hf_l = _res(a_l.reshape(b * s, d), w_o0, xl, lat_mod(0, 2), norm_ffn_g[0], lat_mod(0, 3), lat_mod(0, 4), tm=512)
    xc, hf_c = _res(a_c.reshape(b * cl, d), w_o0, xc, ctx_mod(0, 2), norm_ffn_g[0], ctx_mod(0, 3), ctx_mod(0, 4), tm=512)
    act_l = _swiglu_in(hf_l, w_ffn_in[0])
    act_c = _swiglu_in(hf_c, w_ffn_in[0])
    xl, hm_l = _res(act_l, w_ffn_out[0], xl, lat_mod(0, 5), norm_mix_g[1], lat_mod(1, 0), lat_mod(1, 1), tm=256)
    (hm_c,) = _res(act_c, w_ffn_out[0], xc, ctx_mod(0, 5), norm_mix_g[1], ctx_mod(1, 0), ctx_mod(1, 1),
                   out_x=False, tm=256)

    cq_l, ckv_l, kr_l = _mla_in(hm_l, w_in_p, mla_q_norm_g[0], mla_kv_norm_g[0], *mla_lat_tab)
    _, ckv_c, kr_c = _mla_in(hm_c, w_in_p, mla_q_norm_g[0], mla_kv_norm_g[0], *mla_ctx_tab)
    q_l = _q_up(cq_l, wq_p, mla_lat_tab[0] * MLA_SCALE, mla_lat_tab[1] * MLA_SCALE)
    k_l, v_l = _kv_up(ckv_l, kr_l, wk, wv)
    k_c, v_c = _kv_up(ckv_c, kr_c, wk, wv)
    a_l = _mla_attention(q_l.reshape(b, s, -1),
                         [k_c.reshape(b, cl, -1), k_l.reshape(b, s, -1)],
                         [v_c.reshape(b, cl, -1), v_l.reshape(b, s, -1)], mla_heads, tq=256)
    xl, hf_l = _res(a_l.reshape(b * s, d), w_o1, xl, lat_mod(1, 2), norm_ffn_g[1], lat_mod(1, 3), lat_mod(1, 4), tm=512)
    act_l = _swiglu_in(hf_l, w_ffn_in[1])
    (out,) = _res(act_l, w_ffn_out[1], xl, lat_mod(1, 5), final_norm_g, out_x=False, tm=256)
    return out.reshape(b, s, d)
```

```python
import functools
import math

import jax
import jax.numpy as jnp
from jax import lax
from jax.experimental import pallas as pl
from jax.experimental.pallas import tpu as pltpu

F32 = jnp.float32
BF16 = jnp.bfloat16

GRID_W = 64
NORM_EPS = 1e-6
ROPE_BASE = 10000.0
DA_HEAD_DIM = 128
DA_V_DIM = 2 * DA_HEAD_DIM
DA_SCALE = DA_HEAD_DIM ** -0.5
MLA_NOPE = 128
MLA_ROPE = 64
MLA_V = 128
MLA_Q_LORA = 512
MLA_KV_LORA = 512
MLA_SCALE = (MLA_NOPE + MLA_ROPE) ** -0.5
MLA_QK_PAD = 256
LOG2_E = math.log2(math.e)
DA_Q_SCALE = DA_SCALE * LOG2_E
MLA_Q_SCALE = MLA_SCALE * LOG2_E
ATTN_SUB_ROWS = 256

LANES = 128
V7X_VMEM_BYTES = 64 * 1024 * 1024
VMEM_LIMIT = V7X_VMEM_BYTES - 8 * 1024 * 1024

MOD_ROWS = 16


def _params(*sem):
    return pltpu.CompilerParams(dimension_semantics=sem, vmem_limit_bytes=VMEM_LIMIT)


def _rms(x, g):
    ms = jnp.mean(x * x, axis=-1, keepdims=True)
    return x * lax.rsqrt(ms + NORM_EPS) * g


def _rot_half(x):
    return pltpu.roll(x, LANES // 2, 1)


def _mod_kernel(c_ref, w_ref, b_ref, o_ref):
    c = c_ref[...]
    s = c * jax.nn.sigmoid(c)
    o_ref[0] = jnp.dot(s.astype(BF16), w_ref[0].astype(BF16), preferred_element_type=F32) + b_ref[0]


def _mods(cond, ada_w, ada_b, tn=1024):
    depth, d, n = ada_w.shape
    return pl.pallas_call(
        _mod_kernel,
        grid=(depth, n // tn),
        in_specs=[pl.BlockSpec((MOD_ROWS, d), lambda l, j: (0, 0)),
                  pl.BlockSpec((1, d, tn), lambda l, j: (l, 0, j)),
                  pl.BlockSpec((1, 1, tn), lambda l, j: (l, 0, j))],
        out_specs=pl.BlockSpec((1, MOD_ROWS, tn), lambda l, j: (l, 0, j)),
        out_shape=jax.ShapeDtypeStruct((depth, MOD_ROWS, n), F32),
        compiler_params=_params("arbitrary", "arbitrary"),
        name="adaln_mod",
    )(cond, ada_w, ada_b.reshape(depth, 1, n))


def _norm_mod_kernel(x_ref, g_ref, shift_ref, scale_ref, o_ref):
    y = _rms(x_ref[...], g_ref[...])
    o_ref[...] = (y * (1 + scale_ref[...]) + shift_ref[...]).astype(BF16)


def _group_spec(d, tiles_per_group):
    return pl.BlockSpec((None, 1, d), lambda i, *_: (i // tiles_per_group, 0, 0))


def _norm_mod(x, g, shift, scale, tm=512):
    t, d = x.shape
    tpg = t // shift.shape[0] // tm
    return pl.pallas_call(
        _norm_mod_kernel,
        grid=(t // tm,),
        in_specs=[pl.BlockSpec((tm, d), lambda i: (i, 0)),
                  pl.BlockSpec((1, d), lambda i: (0, 0)),
                  _group_spec(d, tpg), _group_spec(d, tpg)],
        out_specs=pl.BlockSpec((tm, d), lambda i: (i, 0)),
        out_shape=jax.ShapeDtypeStruct((t, d), BF16),
        compiler_params=_params("arbitrary"),
        name="norm_mod",
    )(x, g.reshape(1, d), shift, scale)


def _qkv_rope_kernel(a_ref, w_ref, cos_ref, sin_ref, o_ref, w_bf):
    @pl.when(pl.program_id(1) == 0)
    def _():
        w_bf[...] = w_ref[...].astype(BF16)

    acc = jnp.dot(a_ref[...], w_bf[...], preferred_element_type=F32)
    cos = cos_ref[...]
    sin = sin_ref[...]
    for g in range(acc.shape[1] // LANES):
        sl = slice(g * LANES, (g + 1) * LANES)
        xg = acc[:, sl]
        o_ref[:, sl] = (xg * cos + _rot_half(xg) * sin).astype(BF16)


def _qkv_rope(a, w, layer, cos_tab, sin_tab, tm=1024, tn=1024):
    t, k = a.shape
    n = w.shape[2]
    tiles_per_kind = n // 3 // tn
    row_tiles = cos_tab.shape[1] // tm
    tab_spec = pl.BlockSpec((None, tm, LANES), lambda j, i: (j // tiles_per_kind, i % row_tiles, 0))
    return pl.pallas_call(
        _qkv_rope_kernel,
        grid=(n // tn, t // tm),
        in_specs=[pl.BlockSpec((tm, k), lambda j, i: (i, 0)),
                  pl.BlockSpec((None, k, tn), lambda j, i: (layer, 0, j)),
                  tab_spec, tab_spec],
        out_specs=pl.BlockSpec((tm, tn), lambda j, i: (i, j)),
        out_shape=jax.ShapeDtypeStruct((t, n), BF16),
        scratch_shapes=[pltpu.VMEM((k, tn), BF16)],
        compiler_params=_params("arbitrary", "arbitrary"),
        name="qkv_rope",
    )(a, w, cos_tab, sin_tab)


def _swiglu_kernel(a_ref, wg_ref, wu_ref, o_ref, wg_bf, wu_bf):
    @pl.when(pl.program_id(1) == 0)
    def _():
        wg_bf[...] = wg_ref[...].astype(BF16)
        wu_bf[...] = wu_ref[...].astype(BF16)

    a = a_ref[...]
    g = jnp.dot(a, wg_bf[...], preferred_element_type=F32)
    u = jnp.dot(a, wu_bf[...], preferred_element_type=F32)
    o_ref[...] = (g * jax.nn.sigmoid(g) * u).astype(BF16)


def _swiglu_in(a, w_in, layer, tm=1024, tn=512):
    t, k = a.shape
    hidden = w_in.shape[2] // 2
    nj = hidden // tn
    return pl.pallas_call(
        _swiglu_kernel,
        grid=(nj, t // tm),
        in_specs=[pl.BlockSpec((tm, k), lambda j, i: (i, 0)),
                  pl.BlockSpec((None, k, tn), lambda j, i: (layer, 0, j)),
                  pl.BlockSpec((None, k, tn), lambda j, i: (layer, 0, j + nj))],
        out_specs=pl.BlockSpec((tm, tn), lambda j, i: (i, j)),
        out_shape=jax.ShapeDtypeStruct((t, hidden), BF16),
        scratch_shapes=[pltpu.VMEM((k, tn), BF16), pltpu.VMEM((k, tn), BF16)],
        compiler_params=_params("arbitrary", "arbitrary"),
        name="swiglu_in",
    )(a, w_in, w_in)


def _res_kernel(out_x, next_norm, a_ref, w_ref, x_ref, gate_ref, *refs):
    y = jnp.dot(a_ref[...], w_ref[...], preferred_element_type=F32)
    xn = x_ref[...] + gate_ref[...] * y
    refs = list(refs)
    if next_norm == "mod":
        g_ref, shift_ref, scale_ref = refs[:3]
        outs = refs[3:]
    else:
        g_ref = refs[0]
        outs = refs[1:]
    if out_x:
        outs.pop(0)[...] = xn
    h = _rms(xn, g_ref[...])
    if next_norm == "mod":
        outs[0][...] = (h * (1 + scale_ref[...]) + shift_ref[...]).astype(BF16)
    else:
        outs[0][...] = h


def _res(a, w, layer, x, gate, g, shift=None, scale=None, *, out_x=True, tm):
    t, k = a.shape
    d = w.shape[2]
    next_norm = "final" if shift is None else "mod"
    tpg = t // gate.shape[0] // tm
    row = pl.BlockSpec((tm, d), lambda i: (i, 0))
    in_specs = [pl.BlockSpec((tm, k), lambda i: (i, 0)),
                pl.BlockSpec((None, k, d), lambda i: (layer, 0, 0), pipeline_mode=pl.Buffered(1)),
                row, _group_spec(d, tpg), pl.BlockSpec((1, d), lambda i: (0, 0))]
    args = [a, w, x, gate, g.reshape(1, d)]
    if next_norm == "mod":
        in_specs += [_group_spec(d, tpg), _group_spec(d, tpg)]
        args += [shift, scale]
    out_specs, out_shape = [], []
    if out_x:
        out_specs.append(row)
        out_shape.append(jax.ShapeDtypeStruct((t, d), F32))
    out_specs.append(row)
    out_shape.append(jax.ShapeDtypeStruct((t, d), BF16 if next_norm == "mod" else F32))
    return pl.pallas_call(
        functools.partial(_res_kernel, out_x, next_norm),
        grid=(t // tm,),
        in_specs=in_specs, out_specs=out_specs, out_shape=out_shape,
        compiler_params=_params("arbitrary"),
        name="proj_residual",
    )(*args)


def _nt_dot(q, k):
    return lax.dot_general(q, k, (((1,), (1,)), ((), ())), preferred_element_type=F32)


def _softmax_pv(q, k_parts, v_refs):
    scores = [_nt_dot(q, k) for k in k_parts]
    mx = functools.reduce(jnp.maximum, [jnp.max(s, axis=1, keepdims=True) for s in scores])
    e = [jnp.exp2(s - mx) for s in scores]
    tot = functools.reduce(jnp.add, [jnp.sum(x, axis=1, keepdims=True) for x in e])
    o = functools.reduce(jnp.add, [jnp.dot(p.astype(BF16), v[...], preferred_element_type=F32)
                                   for p, v in zip(e, v_refs)])
    return o, 1.0 / tot


def _da_attn_kernel(nseg, lam_init, lamv_ref, g_ref, q_ref, *refs):
    k_refs, v_refs, o_ref = refs[:nseg], refs[nseg:2 * nseg], refs[2 * nseg]
    lv = lamv_ref[...]
    lam = (jnp.exp(jnp.sum(lv[0:1] * lv[1:2], axis=1, keepdims=True))
           - jnp.exp(jnp.sum(lv[2:3] * lv[3:4], axis=1, keepdims=True)) + lam_init)
    for t in range(q_ref.shape[0] // ATTN_SUB_ROWS):
        rows = slice(t * ATTN_SUB_ROWS, (t + 1) * ATTN_SUB_ROWS)
        o = None
        for m in range(2):
            sl = slice(m * DA_HEAD_DIM, (m + 1) * DA_HEAD_DIM)
            om, r = _softmax_pv(q_ref[rows, sl], [k[:, sl] for k in k_refs], v_refs)
            o = om * r if m == 0 else o - om * (lam * r)
        o_ref[rows, :] = (_rms(o, g_ref[...]) * (1.0 - lam_init)).astype(BF16)


def _da_attention(q_src, kv_srcs, lam_vecs, subln_g, lam_init, heads, tq):
    b, sq, _ = q_src.shape
    hd = DA_V_DIM
    in_specs = [pl.BlockSpec((4, DA_HEAD_DIM), lambda bi, h, qi: (0, 0)),
                pl.BlockSpec((1, hd), lambda bi, h, qi: (0, 0)),
                pl.BlockSpec((None, tq, hd), lambda bi, h, qi: (bi, qi, h))]
    args = [lam_vecs, subln_g.reshape(1, hd), q_src]
    for kind in (1, 2):
        for src in kv_srcs:
            in_specs.append(pl.BlockSpec((None, src.shape[1], hd),
                                         lambda bi, h, qi, kind=kind: (bi, 0, kind * heads + h)))
            args.append(src)
    return pl.pallas_call(
        functools.partial(_da_attn_kernel, len(kv_srcs), lam_init),
        grid=(b, heads, sq // tq),
        in_specs=in_specs,
        out_specs=pl.BlockSpec((None, tq, hd), lambda bi, h, qi: (bi, qi, h)),
        out_shape=jax.ShapeDtypeStruct((b, sq, heads * hd), BF16),
        compiler_params=_params("arbitrary", "arbitrary", "arbitrary"),
        name="diff_attention",
    )(*args)


def _mla_attn_kernel(nseg, q_ref, *refs):
    k_refs, v_refs, o_ref = refs[:nseg], refs[nseg:2 * nseg], refs[2 * nseg]
    for t in range(q_ref.shape[0] // ATTN_SUB_ROWS):
        rows = slice(t * ATTN_SUB_ROWS, (t + 1) * ATTN_SUB_ROWS)
        for hh in range(2):
            cols = slice(hh * MLA_QK_PAD, (hh + 1) * MLA_QK_PAD)
            o, r = _softmax_pv(q_ref[rows, cols], [k[:, cols] for k in k_refs], v_refs)
            half = slice(hh * MLA_V, (hh + 1) * MLA_V)
            o_ref[rows, half] = (o[:, half] * r).astype(BF16)


def _mla_attention(q, ks, vs, heads, tq):
    b, sq, _ = q.shape
    in_specs = [pl.BlockSpec((None, tq, 2 * MLA_QK_PAD), lambda bi, h, qi: (bi, qi, h))]
    in_specs += [pl.BlockSpec((None, k.shape[1], 2 * MLA_QK_PAD), lambda bi, h, qi: (bi, 0, h)) for k in ks]
    in_specs += [pl.BlockSpec((None, v.shape[1], 2 * MLA_V), lambda bi, h, qi: (bi, 0, h)) for v in vs]
    return pl.pallas_call(
        functools.partial(_mla_attn_kernel, len(ks)),
        grid=(b, heads // 2, sq // tq),
        in_specs=in_specs,
        out_specs=pl.BlockSpec((None, tq, 2 * MLA_V), lambda bi, h, qi: (bi, qi, h)),
        out_shape=jax.ShapeDtypeStruct((b, sq, heads * MLA_V), BF16),
        compiler_params=_params("arbitrary", "arbitrary", "arbitrary"),
        name="mla_attention",
    )(q, *ks, *vs)


def _mla_in_kernel(a_ref, w_ref, qg_ref, kvg_ref, cos_ref, sin_ref, cq_ref, ckv_ref, kr_ref):
    acc = jnp.dot(a_ref[...], w_ref[...], preferred_element_type=F32)
    cq_ref[...] = _rms(acc[:, :MLA_Q_LORA], qg_ref[...]).astype(BF16)
    ckv_ref[...] = _rms(acc[:, MLA_Q_LORA:MLA_Q_LORA + MLA_KV_LORA], kvg_ref[...]).astype(BF16)
    kr = acc[:, MLA_Q_LORA + MLA_KV_LORA:]
    kr_ref[...] = (kr * cos_ref[...] + _rot_half(kr) * sin_ref[...]).astype(BF16)


def _mla_in(a, w, q_g, kv_g, cos_tab, sin_tab, tm=512):
    t, k = a.shape
    n = w.shape[1]
    row_tiles = cos_tab.shape[0] // tm
    tab = pl.BlockSpec((tm, LANES), lambda i: (i % row_tiles, 0))
    outs = [(MLA_Q_LORA, BF16), (MLA_KV_LORA, BF16), (LANES, BF16)]
    return pl.pallas_call(
        _mla_in_kernel,
        grid=(t // tm,),
        in_specs=[pl.BlockSpec((tm, k), lambda i: (i, 0)),
                  pl.BlockSpec((k, n), lambda i: (0, 0), pipeline_mode=pl.Buffered(1)),
                  pl.BlockSpec((1, MLA_Q_LORA), lambda i: (0, 0)),
                  pl.BlockSpec((1, MLA_KV_LORA), lambda i: (0, 0)),
                  tab, tab],
        out_specs=[pl.BlockSpec((tm, w_), lambda i: (i, 0)) for w_, _ in outs],
        out_shape=[jax.ShapeDtypeStruct((t, w_), dt) for w_, dt in outs],
        compiler_params=_params("arbitrary"),
        name="mla_down",
    )(a, w, q_g.reshape(1, -1), kv_g.reshape(1, -1), cos_tab, sin_tab)


def _q_up_kernel(a_ref, w_ref, cos_ref, sin_ref, o_ref):
    acc = jnp.dot(a_ref[...], w_ref[...], preferred_element_type=F32)
    cos = cos_ref[...]
    sin = sin_ref[...]
    for h in range(acc.shape[1] // MLA_QK_PAD):
        base = h * MLA_QK_PAD
        o_ref[:, base:base + MLA_NOPE] = (acc[:, base:base + MLA_NOPE] * MLA_Q_SCALE).astype(BF16)
        xr = acc[:, base + MLA_NOPE:base + MLA_QK_PAD]
        o_ref[:, base + MLA_NOPE:base + MLA_QK_PAD] = (xr * cos + _rot_half(xr) * sin).astype(BF16)


def _q_up(a, w, cos_tab, sin_tab, tm=1024, tn=1024):
    t, k = a.shape
    n = w.shape[1]
    row_tiles = cos_tab.shape[0] // tm
    tab = pl.BlockSpec((tm, LANES), lambda i, j: (i % row_tiles, 0))
    return pl.pallas_call(
        _q_up_kernel,
        grid=(t // tm, n // tn),
        in_specs=[pl.BlockSpec((tm, k), lambda i, j: (i, 0)),
                  pl.BlockSpec((k, tn), lambda i, j: (0, j)),
                  tab, tab],
        out_specs=pl.BlockSpec((tm, tn), lambda i, j: (i, j)),
        out_shape=jax.ShapeDtypeStruct((t, n), BF16),
        compiler_params=_params("arbitrary", "arbitrary"),
        name="mla_q_up",
    )(a, w, cos_tab, sin_tab)


def _kv_up_kernel(a_ref, kr_ref, wk_ref, wv_ref, k_ref, v_ref):
    a = a_ref[...]
    kn = jnp.dot(a, wk_ref[...], preferred_element_type=F32).astype(BF16)
    v_ref[...] = jnp.dot(a, wv_ref[...], preferred_element_type=F32).astype(BF16)
    kr = kr_ref[...]
    for h in range(kn.shape[1] // MLA_NOPE):
        base = h * MLA_QK_PAD
        k_ref[:, base:base + MLA_NOPE] = kn[:, h * MLA_NOPE:(h + 1) * MLA_NOPE]
        k_ref[:, base + MLA_NOPE:base + MLA_QK_PAD] = kr


def _kv_up(a, kr, wk, wv, tm=512):
    t, k = a.shape
    nk, nv = wk.shape[1], wv.shape[1]
    heads = nk // MLA_NOPE
    return pl.pallas_call(
        _kv_up_kernel,
        grid=(t // tm,),
        in_specs=[pl.BlockSpec((tm, k), lambda i: (i, 0)),
                  pl.BlockSpec((tm, LANES), lambda i: (i, 0)),
                  pl.BlockSpec((k, nk), lambda i: (0, 0), pipeline_mode=pl.Buffered(1)),
                  pl.BlockSpec((k, nv), lambda i: (0, 0), pipeline_mode=pl.Buffered(1))],
        out_specs=[pl.BlockSpec((tm, heads * MLA_QK_PAD), lambda i: (i, 0)),
                   pl.BlockSpec((tm, nv), lambda i: (i, 0))],
        out_shape=[jax.ShapeDtypeStruct((t, heads * MLA_QK_PAD), BF16),
                   jax.ShapeDtypeStruct((t, nv), BF16)],
        compiler_params=_params("arbitrary"),
        name="mla_kv_up",
    )(a, kr, wk, wv)


def _axial_angles(rows, rot_dim):
    row = jnp.repeat(jnp.arange(rows, dtype=F32), GRID_W)
    col = jnp.tile(jnp.arange(GRID_W, dtype=F32), rows)
    n_freq = rot_dim // 4
    inv_freq = ROPE_BASE ** (-jnp.arange(n_freq, dtype=F32) / n_freq)
    ang = jnp.concatenate([row[:, None] * inv_freq, col[:, None] * inv_freq], axis=-1)
    return jnp.cos(ang), jnp.sin(ang)


def _da_tables(seq, ctx_rows):
    cos, sin = _axial_angles(seq // GRID_W, DA_HEAD_DIM)
    cos = jnp.concatenate([cos, cos], axis=-1)
    sin = jnp.concatenate([-sin, sin], axis=-1)
    one, zero = jnp.ones_like(cos), jnp.zeros_like(cos)
    lat = (jnp.stack([cos * DA_Q_SCALE, cos, one]), jnp.stack([sin * DA_Q_SCALE, sin, zero]))
    one_c, zero_c = one[:ctx_rows], zero[:ctx_rows]
    ctx = (jnp.stack([one_c * DA_Q_SCALE, one_c, one_c]), jnp.stack([zero_c, zero_c, zero_c]))
    return lat, ctx


def _spread(x1, x2):
    z = jnp.zeros_like(x1)
    return jnp.concatenate([x1, z, x2, z], axis=-1)


def _mla_tables(seq, ctx_rows):
    cos, sin = _axial_angles(seq // GRID_W, MLA_ROPE)
    cos_l, sin_l = _spread(cos, cos), _spread(-sin, sin)
    cos_c, sin_c = jnp.ones_like(cos_l[:ctx_rows]), jnp.zeros_like(cos_l[:ctx_rows])
    return (cos_l, sin_l), (cos_c, sin_c)


def _mla_weight_layouts(w_in, w_q_up, w_kv_up):
    half = MLA_ROPE // 2
    lora = MLA_Q_LORA + MLA_KV_LORA
    w_in_p = jnp.concatenate([w_in[:, :lora], _spread(w_in[:, lora:lora + half], w_in[:, lora + half:])], axis=1)
    wq = w_q_up.reshape(MLA_Q_LORA, -1, MLA_NOPE + MLA_ROPE)
    wq_p = jnp.concatenate([wq[..., :MLA_NOPE],
                            _spread(wq[..., MLA_NOPE:MLA_NOPE + half], wq[..., MLA_NOPE + half:])], axis=-1)
    wkv = w_kv_up.reshape(MLA_KV_LORA, -1, MLA_NOPE + MLA_V)
    wk = wkv[..., :MLA_NOPE].reshape(MLA_KV_LORA, -1)
    wv = wkv[..., MLA_NOPE:].reshape(MLA_KV_LORA, -1)
    return w_in_p, wq_p.reshape(MLA_Q_LORA, -1), wk, wv


def kernel(x, c, ctx, c_ctx, ada_w, ada_b, norm_mix_g, norm_ffn_g, ffn_w_in, ffn_w_out, da_w_qkv, da_lambda, da_subln_g, da_w_o, mla_w_in, mla_q_norm_g, mla_w_q_up, mla_kv_norm_g, mla_w_kv_up, mla_w_o, final_norm_g):
    b, s, d = x.shape
    cl = ctx.shape[1]
    assert ada_w.shape[0] == 2 and b + 1 <= MOD_ROWS
    da_heads = d // DA_V_DIM
    mla_heads = d // MLA_V

    cond = jnp.concatenate([c, c_ctx[None], jnp.zeros((MOD_ROWS - b - 1, d), F32)], axis=0)
    mods = _mods(cond, ada_w, ada_b)

    def lat_mod(i, k):
        return mods[i, :b, k * d:(k + 1) * d].reshape(b, 1, d)

    def ctx_mod(i, k):
        return mods[i, b:b + 1, k * d:(k + 1) * d].reshape(1, 1, d)

    bf = lambda w: w.astype(BF16)
    w_o0, w_o1, w_ffn_out = bf(da_w_o), bf(mla_w_o), bf(ffn_w_out)
    w_in_p, wq_p, wk, wv = (bf(w) for w in _mla_weight_layouts(mla_w_in[0], mla_w_q_up[0], mla_w_kv_up[0]))
    da_lat_tab, da_ctx_tab = _da_tables(s, 1024)
    mla_lat_tab, mla_ctx_tab = _mla_tables(s, 512)

    xl = x.reshape(b * s, d)
    xc = ctx.reshape(b * cl, d)

    lam_init = 0.8 - 0.6 * math.exp(-0.3 * 0)
    h_l = _norm_mod(xl, norm_mix_g[0], lat_mod(0, 0), lat_mod(0, 1))
    h_c = _norm_mod(xc, norm_mix_g[0], ctx_mod(0, 0), ctx_mod(0, 1))
    qkv_l = _qkv_rope(h_l, da_w_qkv, 0, *da_lat_tab).reshape(b, s, -1)
    qkv_c = _qkv_rope(h_c, da_w_qkv, 0, *da_ctx_tab).reshape(b, cl, -1)
    a_l = _da_attention(qkv_l, [qkv_c, qkv_l], da_lambda[0], da_subln_g[0], lam_init, da_heads, tq=1024)
    a_c = _da_attention(qkv_c, [qkv_c], da_lambda[0], da_subln_g[0], lam_init, da_heads, tq=cl)
    xl, hf_l = _res(a_l.reshape(b * s, d), w_o0, 0, xl, lat_mod(0, 2), norm_ffn_g[0], lat_mod(0, 3), lat_mod(0, 4), tm=512)
    xc, hf_c = _res(a_c.reshape(b * cl, d), w_o0, 0, xc, ctx_mod(0, 2), norm_ffn_g[0], ctx_mod(0, 3), ctx_mod(0, 4), tm=512)
    act_l = _swiglu_in(hf_l, ffn_w_in, 0)
    act_c = _swiglu_in(hf_c, ffn_w_in, 0)
    xl, hm_l = _res(act_l, w_ffn_out, 0, xl, lat_mod(0, 5), norm_mix_g[1], lat_mod(1, 0), lat_mod(1, 1), tm=256)
    (hm_c,) = _res(act_c, w_ffn_out, 0, xc, ctx_mod(0, 5), norm_mix_g[1], ctx_mod(1, 0), ctx_mod(1, 1),
                   out_x=False, tm=256)

    cq_l, ckv_l, kr_l = _mla_in(hm_l, w_in_p, mla_q_norm_g[0], mla_kv_norm_g[0], *mla_lat_tab)
    _, ckv_c, kr_c = _mla_in(hm_c, w_in_p, mla_q_norm_g[0], mla_kv_norm_g[0], *mla_ctx_tab)
    q_l = _q_up(cq_l, wq_p, mla_lat_tab[0] * MLA_Q_SCALE, mla_lat_tab[1] * MLA_Q_SCALE)
    k_l, v_l = _kv_up(ckv_l, kr_l, wk, wv)
    k_c, v_c = _kv_up(ckv_c, kr_c, wk, wv)
    a_l = _mla_attention(q_l.reshape(b, s, -1),
                         [k_c.reshape(b, cl, -1), k_l.reshape(b, s, -1)],
                         [v_c.reshape(b, cl, -1), v_l.reshape(b, s, -1)], mla_heads, tq=1024)
    xl, hf_l = _res(a_l.reshape(b * s, d), w_o1, 0, xl, lat_mod(1, 2), norm_ffn_g[1], lat_mod(1, 3), lat_mod(1, 4), tm=512)
    act_l = _swiglu_in(hf_l, ffn_w_in, 1)
    (out,) = _res(act_l, w_ffn_out, 1, xl, lat_mod(1, 5), final_norm_g, out_x=False, tm=256)
    return out.reshape(b, s, d)
```

```python
import functools
import math

import jax
import jax.numpy as jnp
from jax import lax
from jax.experimental import pallas as pl
from jax.experimental.pallas import tpu as pltpu

F32 = jnp.float32
BF16 = jnp.bfloat16

GRID_W = 64
NORM_EPS = 1e-6
ROPE_BASE = 10000.0
DA_HEAD_DIM = 128
DA_V_DIM = 2 * DA_HEAD_DIM
DA_SCALE = DA_HEAD_DIM ** -0.5
MLA_NOPE = 128
MLA_ROPE = 64
MLA_V = 128
MLA_Q_LORA = 512
MLA_KV_LORA = 512
MLA_SCALE = (MLA_NOPE + MLA_ROPE) ** -0.5
MLA_QK_PAD = 256
LOG2_E = math.log2(math.e)
DA_Q_SCALE = DA_SCALE * LOG2_E
MLA_Q_SCALE = MLA_SCALE * LOG2_E
ATTN_SUB_ROWS = 256
MXU_COLS = 256
MM_SUB_COLS = 2 * MXU_COLS
RES_SUB_ROWS = 256
Q_UP_SUB_COLS = 1024

LANES = 128
V7X_VMEM_BYTES = 64 * 1024 * 1024
VMEM_LIMIT = V7X_VMEM_BYTES - 8 * 1024 * 1024

MOD_ROWS = 16


def _params(*sem):
    return pltpu.CompilerParams(dimension_semantics=sem, vmem_limit_bytes=VMEM_LIMIT)


def _rms(x, g):
    ms = jnp.mean(x * x, axis=-1, keepdims=True)
    return x * lax.rsqrt(ms + NORM_EPS) * g


def _rot_half(x):
    return pltpu.roll(x, LANES // 2, 1)


def _mod_kernel(c_ref, w_ref, b_ref, o_ref):
    c = c_ref[...]
    s = c * jax.nn.sigmoid(c)
    o_ref[0] = jnp.dot(s.astype(BF16), w_ref[0].astype(BF16), preferred_element_type=F32) + b_ref[0]


def _mods(cond, ada_w, ada_b, tn=1024):
    depth, d, n = ada_w.shape
    return pl.pallas_call(
        _mod_kernel,
        grid=(depth, n // tn),
        in_specs=[pl.BlockSpec((MOD_ROWS, d), lambda l, j: (0, 0)),
                  pl.BlockSpec((1, d, tn), lambda l, j: (l, 0, j)),
                  pl.BlockSpec((1, 1, tn), lambda l, j: (l, 0, j))],
        out_specs=pl.BlockSpec((1, MOD_ROWS, tn), lambda l, j: (l, 0, j)),
        out_shape=jax.ShapeDtypeStruct((depth, MOD_ROWS, n), F32),
        compiler_params=_params("arbitrary", "arbitrary"),
        name="adaln_mod",
    )(cond, ada_w, ada_b.reshape(depth, 1, n))


def _norm_mod_kernel(x_ref, g_ref, shift_ref, scale_ref, o_ref):
    y = _rms(x_ref[...], g_ref[...])
    o_ref[...] = (y * (1 + scale_ref[...]) + shift_ref[...]).astype(BF16)


def _group_spec(d, tiles_per_group):
    return pl.BlockSpec((None, 1, d), lambda i, *_: (i // tiles_per_group, 0, 0))


def _norm_mod(x, g, shift, scale, tm=512):
    t, d = x.shape
    tpg = t // shift.shape[0] // tm
    return pl.pallas_call(
        _norm_mod_kernel,
        grid=(t // tm,),
        in_specs=[pl.BlockSpec((tm, d), lambda i: (i, 0)),
                  pl.BlockSpec((1, d), lambda i: (0, 0)),
                  _group_spec(d, tpg), _group_spec(d, tpg)],
        out_specs=pl.BlockSpec((tm, d), lambda i: (i, 0)),
        out_shape=jax.ShapeDtypeStruct((t, d), BF16),
        compiler_params=_params("arbitrary"),
        name="norm_mod",
    )(x, g.reshape(1, d), shift, scale)


def _qkv_rope_kernel(a_ref, w_ref, cos_ref, sin_ref, o_ref, w_bf):
    @pl.when(pl.program_id(1) == 0)
    def _():
        w_bf[...] = w_ref[...].astype(BF16)

    a = a_ref[...]
    cos = cos_ref[...]
    sin = sin_ref[...]
    for c in range(w_bf.shape[1] // MM_SUB_COLS):
        acc = jnp.dot(a, w_bf[:, c * MM_SUB_COLS:(c + 1) * MM_SUB_COLS], preferred_element_type=F32)
        for g in range(MM_SUB_COLS // LANES):
            xg = acc[:, g * LANES:(g + 1) * LANES]
            lo = c * MM_SUB_COLS + g * LANES
            o_ref[:, lo:lo + LANES] = (xg * cos + _rot_half(xg) * sin).astype(BF16)


def _qkv_rope(a, w, layer, cos_tab, sin_tab, tm=1024, tn=1024):
    t, k = a.shape
    n = w.shape[2]
    tiles_per_kind = n // 3 // tn
    row_tiles = cos_tab.shape[1] // tm
    tab_spec = pl.BlockSpec((None, tm, LANES), lambda j, i: (j // tiles_per_kind, i % row_tiles, 0))
    return pl.pallas_call(
        _qkv_rope_kernel,
        grid=(n // tn, t // tm),
        in_specs=[pl.BlockSpec((tm, k), lambda j, i: (i, 0)),
                  pl.BlockSpec((None, k, tn), lambda j, i: (layer, 0, j)),
                  tab_spec, tab_spec],
        out_specs=pl.BlockSpec((tm, tn), lambda j, i: (i, j)),
        out_shape=jax.ShapeDtypeStruct((t, n), BF16),
        scratch_shapes=[pltpu.VMEM((k, tn), BF16)],
        compiler_params=_params("arbitrary", "arbitrary"),
        name="qkv_rope",
    )(a, w, cos_tab, sin_tab)


def _swiglu_kernel(a_ref, wg_ref, wu_ref, o_ref, wg_bf, wu_bf):
    @pl.when(pl.program_id(1) == 0)
    def _():
        wg_bf[...] = wg_ref[...].astype(BF16)
        wu_bf[...] = wu_ref[...].astype(BF16)

    a = a_ref[...]
    for c in range(wg_bf.shape[1] // MXU_COLS):
        cols = slice(c * MXU_COLS, (c + 1) * MXU_COLS)
        g = jnp.dot(a, wg_bf[:, cols], preferred_element_type=F32)
        u = jnp.dot(a, wu_bf[:, cols], preferred_element_type=F32)
        o_ref[:, cols] = (g * jax.nn.sigmoid(g) * u).astype(BF16)


def _swiglu_in(a, w_in, layer, tm=1024, tn=512):
    t, k = a.shape
    hidden = w_in.shape[2] // 2
    nj = hidden // tn
    return pl.pallas_call(
        _swiglu_kernel,
        grid=(nj, t // tm),
        in_specs=[pl.BlockSpec((tm, k), lambda j, i: (i, 0)),
                  pl.BlockSpec((None, k, tn), lambda j, i: (layer, 0, j)),
                  pl.BlockSpec((None, k, tn), lambda j, i: (layer, 0, j + nj))],
        out_specs=pl.BlockSpec((tm, tn), lambda j, i: (i, j)),
        out_shape=jax.ShapeDtypeStruct((t, hidden), BF16),
        scratch_shapes=[pltpu.VMEM((k, tn), BF16), pltpu.VMEM((k, tn), BF16)],
        compiler_params=_params("arbitrary", "arbitrary"),
        name="swiglu_in",
    )(a, w_in, w_in)


def _res_kernel(out_x, next_norm, a_ref, w_ref, x_ref, gate_ref, *refs):
    refs = list(refs)
    if next_norm == "mod":
        g_ref, shift_ref, scale_ref = refs[:3]
        outs = refs[3:]
    else:
        g_ref = refs[0]
        outs = refs[1:]
    ox_ref = outs.pop(0) if out_x else None
    oh_ref = outs[0]
    for r in range(a_ref.shape[0] // RES_SUB_ROWS):
        rows = slice(r * RES_SUB_ROWS, (r + 1) * RES_SUB_ROWS)
        y = jnp.dot(a_ref[rows, :], w_ref[...], preferred_element_type=F32)
        xn = x_ref[rows, :] + gate_ref[...] * y
        if out_x:
            ox_ref[rows, :] = xn
        h = _rms(xn, g_ref[...])
        if next_norm == "mod":
            oh_ref[rows, :] = (h * (1 + scale_ref[...]) + shift_ref[...]).astype(BF16)
        else:
            oh_ref[rows, :] = h


def _res(a, w, layer, x, gate, g, shift=None, scale=None, *, out_x=True, tm):
    t, k = a.shape
    d = w.shape[2]
    next_norm = "final" if shift is None else "mod"
    tpg = t // gate.shape[0] // tm
    row = pl.BlockSpec((tm, d), lambda i: (i, 0))
    in_specs = [pl.BlockSpec((tm, k), lambda i: (i, 0)),
                pl.BlockSpec((None, k, d), lambda i: (layer, 0, 0), pipeline_mode=pl.Buffered(1)),
                row, _group_spec(d, tpg), pl.BlockSpec((1, d), lambda i: (0, 0))]
    args = [a, w, x, gate, g.reshape(1, d)]
    if next_norm == "mod":
        in_specs += [_group_spec(d, tpg), _group_spec(d, tpg)]
        args += [shift, scale]
    out_specs, out_shape = [], []
    if out_x:
        out_specs.append(row)
        out_shape.append(jax.ShapeDtypeStruct((t, d), F32))
    out_specs.append(row)
    out_shape.append(jax.ShapeDtypeStruct((t, d), BF16 if next_norm == "mod" else F32))
    return pl.pallas_call(
        functools.partial(_res_kernel, out_x, next_norm),
        grid=(t // tm,),
        in_specs=in_specs, out_specs=out_specs, out_shape=out_shape,
        compiler_params=_params("arbitrary"),
        name="proj_residual",
    )(*args)


def _nt_dot(q, k):
    return lax.dot_general(q, k, (((1,), (1,)), ((), ())), preferred_element_type=F32)


def _softmax_pv(q, k_parts, v_refs):
    scores = [_nt_dot(q, k) for k in k_parts]
    mx = functools.reduce(jnp.maximum, [jnp.max(s, axis=1, keepdims=True) for s in scores])
    e = [jnp.exp2(s - mx) for s in scores]
    tot = functools.reduce(jnp.add, [jnp.sum(x, axis=1, keepdims=True) for x in e])
    o = functools.reduce(jnp.add, [jnp.dot(p.astype(BF16), v[...], preferred_element_type=F32)
                                   for p, v in zip(e, v_refs)])
    return o, 1.0 / tot


def _da_attn_kernel(nseg, lam_init, lamv_ref, g_ref, q_ref, *refs):
    k_refs, v_refs, o_ref = refs[:nseg], refs[nseg:2 * nseg], refs[2 * nseg]
    lv = lamv_ref[...]
    lam = (jnp.exp(jnp.sum(lv[0:1] * lv[1:2], axis=1, keepdims=True))
           - jnp.exp(jnp.sum(lv[2:3] * lv[3:4], axis=1, keepdims=True)) + lam_init)
    for t in range(q_ref.shape[0] // ATTN_SUB_ROWS):
        rows = slice(t * ATTN_SUB_ROWS, (t + 1) * ATTN_SUB_ROWS)
        o = None
        for m in range(2):
            sl = slice(m * DA_HEAD_DIM, (m + 1) * DA_HEAD_DIM)
            om, r = _softmax_pv(q_ref[rows, sl], [k[:, sl] for k in k_refs], v_refs)
            o = om * r if m == 0 else o - om * (lam * r)
        o_ref[rows, :] = (_rms(o, g_ref[...]) * (1.0 - lam_init)).astype(BF16)


def _da_attention(q_src, kv_srcs, lam_vecs, subln_g, lam_init, heads, tq):
    b, sq, _ = q_src.shape
    hd = DA_V_DIM
    in_specs = [pl.BlockSpec((4, DA_HEAD_DIM), lambda bi, h, qi: (0, 0)),
                pl.BlockSpec((1, hd), lambda bi, h, qi: (0, 0)),
                pl.BlockSpec((None, tq, hd), lambda bi, h, qi: (bi, qi, h))]
    args = [lam_vecs, subln_g.reshape(1, hd), q_src]
    for kind in (1, 2):
        for src in kv_srcs:
            in_specs.append(pl.BlockSpec((None, src.shape[1], hd),
                                         lambda bi, h, qi, kind=kind: (bi, 0, kind * heads + h)))
            args.append(src)
    return pl.pallas_call(
        functools.partial(_da_attn_kernel, len(kv_srcs), lam_init),
        grid=(b, heads, sq // tq),
        in_specs=in_specs,
        out_specs=pl.BlockSpec((None, tq, hd), lambda bi, h, qi: (bi, qi, h)),
        out_shape=jax.ShapeDtypeStruct((b, sq, heads * hd), BF16),
        compiler_params=_params("arbitrary", "arbitrary", "arbitrary"),
        name="diff_attention",
    )(*args)


def _mla_attn_kernel(nseg, q_ref, *refs):
    k_refs, v_refs, o_ref = refs[:nseg], refs[nseg:2 * nseg], refs[2 * nseg]
    for t in range(q_ref.shape[0] // ATTN_SUB_ROWS):
        rows = slice(t * ATTN_SUB_ROWS, (t + 1) * ATTN_SUB_ROWS)
        for hh in range(2):
            cols = slice(hh * MLA_QK_PAD, (hh + 1) * MLA_QK_PAD)
            o, r = _softmax_pv(q_ref[rows, cols], [k[:, cols] for k in k_refs], v_refs)
            half = slice(hh * MLA_V, (hh + 1) * MLA_V)
            o_ref[rows, half] = (o[:, half] * r).astype(BF16)


def _mla_attention(q, ks, vs, heads, tq):
    b, sq, _ = q.shape
    in_specs = [pl.BlockSpec((None, tq, 2 * MLA_QK_PAD), lambda bi, h, qi: (bi, qi, h))]
    in_specs += [pl.BlockSpec((None, k.shape[1], 2 * MLA_QK_PAD), lambda bi, h, qi: (bi, 0, h)) for k in ks]
    in_specs += [pl.BlockSpec((None, v.shape[1], 2 * MLA_V), lambda bi, h, qi: (bi, 0, h)) for v in vs]
    return pl.pallas_call(
        functools.partial(_mla_attn_kernel, len(ks)),
        grid=(b, heads // 2, sq // tq),
        in_specs=in_specs,
        out_specs=pl.BlockSpec((None, tq, 2 * MLA_V), lambda bi, h, qi: (bi, qi, h)),
        out_shape=jax.ShapeDtypeStruct((b, sq, heads * MLA_V), BF16),
        compiler_params=_params("arbitrary", "arbitrary", "arbitrary"),
        name="mla_attention",
    )(q, *ks, *vs)


def _mla_proj_kernel(with_q, a_ref, win_ref, qg_ref, kvg_ref, cos_ref, sin_ref, wk_ref, wv_ref, *refs):
    if with_q:
        wq_ref, q_ref, k_ref, v_ref = refs
    else:
        k_ref, v_ref = refs
    cos = cos_ref[...]
    sin = sin_ref[...]
    acc = jnp.dot(a_ref[...], win_ref[...], preferred_element_type=F32)
    ckv = _rms(acc[:, MLA_Q_LORA:MLA_Q_LORA + MLA_KV_LORA], kvg_ref[...]).astype(BF16)
    kr = acc[:, MLA_Q_LORA + MLA_KV_LORA:]
    kr = (kr * cos + _rot_half(kr) * sin).astype(BF16)
    v_ref[...] = jnp.dot(ckv, wv_ref[...], preferred_element_type=F32).astype(BF16)
    kn = jnp.dot(ckv, wk_ref[...], preferred_element_type=F32).astype(BF16)
    for h in range(kn.shape[1] // MLA_NOPE):
        base = h * MLA_QK_PAD
        k_ref[:, base:base + MLA_NOPE] = kn[:, h * MLA_NOPE:(h + 1) * MLA_NOPE]
        k_ref[:, base + MLA_NOPE:base + MLA_QK_PAD] = kr
    if with_q:
        cq = _rms(acc[:, :MLA_Q_LORA], qg_ref[...]).astype(BF16)
        qcos = cos * MLA_Q_SCALE
        qsin = sin * MLA_Q_SCALE
        for c in range(wq_ref.shape[1] // Q_UP_SUB_COLS):
            qa = jnp.dot(cq, wq_ref[:, c * Q_UP_SUB_COLS:(c + 1) * Q_UP_SUB_COLS], preferred_element_type=F32)
            for h in range(Q_UP_SUB_COLS // MLA_QK_PAD):
                lo = h * MLA_QK_PAD
                base = c * Q_UP_SUB_COLS + lo
                q_ref[:, base:base + MLA_NOPE] = (qa[:, lo:lo + MLA_NOPE] * MLA_Q_SCALE).astype(BF16)
                xr = qa[:, lo + MLA_NOPE:lo + MLA_QK_PAD]
                q_ref[:, base + MLA_NOPE:base + MLA_QK_PAD] = (xr * qcos + _rot_half(xr) * qsin).astype(BF16)


def _mla_proj(a, w_in, q_g, kv_g, cos_tab, sin_tab, wk, wv, wq=None, tm=512):
    t, k = a.shape
    with_q = wq is not None
    row_tiles = cos_tab.shape[0] // tm
    heads = wk.shape[1] // MLA_NOPE
    const = lambda arr: pl.BlockSpec(arr.shape, lambda i: (0, 0), pipeline_mode=pl.Buffered(1))
    tab = pl.BlockSpec((tm, LANES), lambda i: (i % row_tiles, 0))
    in_specs = [pl.BlockSpec((tm, k), lambda i: (i, 0)), const(w_in),
                pl.BlockSpec((1, MLA_Q_LORA), lambda i: (0, 0)),
                pl.BlockSpec((1, MLA_KV_LORA), lambda i: (0, 0)),
                tab, tab, const(wk), const(wv)]
    args = [a, w_in, q_g.reshape(1, -1), kv_g.reshape(1, -1), cos_tab, sin_tab, wk, wv]
    widths = [heads * MLA_QK_PAD, wv.shape[1]]
    if with_q:
        in_specs.append(const(wq))
        args.append(wq)
        widths.insert(0, wq.shape[1])
    return pl.pallas_call(
        functools.partial(_mla_proj_kernel, with_q),
        grid=(t // tm,),
        in_specs=in_specs,
        out_specs=[pl.BlockSpec((tm, w_), lambda i: (i, 0)) for w_ in widths],
        out_shape=[jax.ShapeDtypeStruct((t, w_), BF16) for w_ in widths],
        compiler_params=_params("arbitrary"),
        name="mla_proj",
    )(*args)


def _axial_angles(rows, rot_dim):
    row = jnp.repeat(jnp.arange(rows, dtype=F32), GRID_W)
    col = jnp.tile(jnp.arange(GRID_W, dtype=F32), rows)
    n_freq = rot_dim // 4
    inv_freq = ROPE_BASE ** (-jnp.arange(n_freq, dtype=F32) / n_freq)
    ang = jnp.concatenate([row[:, None] * inv_freq, col[:, None] * inv_freq], axis=-1)
    return jnp.cos(ang), jnp.sin(ang)


def _da_tables(seq, ctx_rows):
    cos, sin = _axial_angles(seq // GRID_W, DA_HEAD_DIM)
    cos = jnp.concatenate([cos, cos], axis=-1)
    sin = jnp.concatenate([-sin, sin], axis=-1)
    one, zero = jnp.ones_like(cos), jnp.zeros_like(cos)
    lat = (jnp.stack([cos * DA_Q_SCALE, cos, one]), jnp.stack([sin * DA_Q_SCALE, sin, zero]))
    one_c, zero_c = one[:ctx_rows], zero[:ctx_rows]
    ctx = (jnp.stack([one_c * DA_Q_SCALE, one_c, one_c]), jnp.stack([zero_c, zero_c, zero_c]))
    return lat, ctx


def _spread(x1, x2):
    z = jnp.zeros_like(x1)
    return jnp.concatenate([x1, z, x2, z], axis=-1)


def _mla_tables(seq, ctx_rows):
    cos, sin = _axial_angles(seq // GRID_W, MLA_ROPE)
    cos_l, sin_l = _spread(cos, cos), _spread(-sin, sin)
    cos_c, sin_c = jnp.ones_like(cos_l[:ctx_rows]), jnp.zeros_like(cos_l[:ctx_rows])
    return (cos_l, sin_l), (cos_c, sin_c)


def _mla_weight_layouts(w_in, w_q_up, w_kv_up):
    half = MLA_ROPE // 2
    lora = MLA_Q_LORA + MLA_KV_LORA
    w_in_p = jnp.concatenate([w_in[:, :lora], _spread(w_in[:, lora:lora + half], w_in[:, lora + half:])], axis=1)
    wq = w_q_up.reshape(MLA_Q_LORA, -1, MLA_NOPE + MLA_ROPE)
    wq_p = jnp.concatenate([wq[..., :MLA_NOPE],
                            _spread(wq[..., MLA_NOPE:MLA_NOPE + half], wq[..., MLA_NOPE + half:])], axis=-1)
    wkv = w_kv_up.reshape(MLA_KV_LORA, -1, MLA_NOPE + MLA_V)
    wk = wkv[..., :MLA_NOPE].reshape(MLA_KV_LORA, -1)
    wv = wkv[..., MLA_NOPE:].reshape(MLA_KV_LORA, -1)
    return w_in_p, wq_p.reshape(MLA_Q_LORA, -1), wk, wv


def kernel(x, c, ctx, c_ctx, ada_w, ada_b, norm_mix_g, norm_ffn_g, ffn_w_in, ffn_w_out, da_w_qkv, da_lambda, da_subln_g, da_w_o, mla_w_in, mla_q_norm_g, mla_w_q_up, mla_kv_norm_g, mla_w_kv_up, mla_w_o, final_norm_g):
    b, s, d = x.shape
    cl = ctx.shape[1]
    assert ada_w.shape[0] == 2 and b + 1 <= MOD_ROWS
    da_heads = d // DA_V_DIM
    mla_heads = d // MLA_V

    cond = jnp.concatenate([c, c_ctx[None], jnp.zeros((MOD_ROWS - b - 1, d), F32)], axis=0)
    mods = _mods(cond, ada_w, ada_b)

    def lat_mod(i, k):
        return mods[i, :b, k * d:(k + 1) * d].reshape(b, 1, d)

    def ctx_mod(i, k):
        return mods[i, b:b + 1, k * d:(k + 1) * d].reshape(1, 1, d)

    bf = lambda w: w.astype(BF16)
    w_o0, w_o1, w_ffn_out = bf(da_w_o), bf(mla_w_o), bf(ffn_w_out)
    w_in_p, wq_p, wk, wv = (bf(w) for w in _mla_weight_layouts(mla_w_in[0], mla_w_q_up[0], mla_w_kv_up[0]))
    da_lat_tab, da_ctx_tab = _da_tables(s, 1024)
    mla_lat_tab, mla_ctx_tab = _mla_tables(s, 512)

    xl = x.reshape(b * s, d)
    xc = ctx.reshape(b * cl, d)

    lam_init = 0.8 - 0.6 * math.exp(-0.3 * 0)
    h_l = _norm_mod(xl, norm_mix_g[0], lat_mod(0, 0), lat_mod(0, 1))
    h_c = _norm_mod(xc, norm_mix_g[0], ctx_mod(0, 0), ctx_mod(0, 1))
    qkv_l = _qkv_rope(h_l, da_w_qkv, 0, *da_lat_tab).reshape(b, s, -1)
    qkv_c = _qkv_rope(h_c, da_w_qkv, 0, *da_ctx_tab).reshape(b, cl, -1)
    a_l = _da_attention(qkv_l, [qkv_c, qkv_l], da_lambda[0], da_subln_g[0], lam_init, da_heads, tq=s)
    a_c = _da_attention(qkv_c, [qkv_c], da_lambda[0], da_subln_g[0], lam_init, da_heads, tq=cl)
    xl, hf_l = _res(a_l.reshape(b * s, d), w_o0, 0, xl, lat_mod(0, 2), norm_ffn_g[0], lat_mod(0, 3), lat_mod(0, 4), tm=512)
    xc, hf_c = _res(a_c.reshape(b * cl, d), w_o0, 0, xc, ctx_mod(0, 2), norm_ffn_g[0], ctx_mod(0, 3), ctx_mod(0, 4), tm=512)
    act_l = _swiglu_in(hf_l, ffn_w_in, 0, tm=2048)
    act_c = _swiglu_in(hf_c, ffn_w_in, 0)
    xl, hm_l = _res(act_l, w_ffn_out, 0, xl, lat_mod(0, 5), norm_mix_g[1], lat_mod(1, 0), lat_mod(1, 1), tm=256)
    (hm_c,) = _res(act_c, w_ffn_out, 0, xc, ctx_mod(0, 5), norm_mix_g[1], ctx_mod(1, 0), ctx_mod(1, 1),
                   out_x=False, tm=256)

    q_l, k_l, v_l = _mla_proj(hm_l, w_in_p, mla_q_norm_g[0], mla_kv_norm_g[0], *mla_lat_tab, wk, wv, wq_p)
    k_c, v_c = _mla_proj(hm_c, w_in_p, mla_q_norm_g[0], mla_kv_norm_g[0], *mla_ctx_tab, wk, wv)
    a_l = _mla_attention(q_l.reshape(b, s, -1),
                         [k_c.reshape(b, cl, -1), k_l.reshape(b, s, -1)],
                         [v_c.reshape(b, cl, -1), v_l.reshape(b, s, -1)], mla_heads, tq=s)
    xl, hf_l = _res(a_l.reshape(b * s, d), w_o1, 0, xl, lat_mod(1, 2), norm_ffn_g[1], lat_mod(1, 3), lat_mod(1, 4), tm=512)
    act_l = _swiglu_in(hf_l, ffn_w_in, 1, tm=2048)
    (out,) = _res(act_l, w_ffn_out, 1, xl, lat_mod(1, 5), final_norm_g, out_x=False, tm=256)
    return out.reshape(b, s, d)
```

```python
import functools
import math
from typing import NamedTuple

import jax
import jax.numpy as jnp
from jax import lax
from jax.experimental import pallas as pl
from jax.experimental.pallas import tpu as pltpu

F32 = jnp.float32
BF16 = jnp.bfloat16

GRID_W = 64
NORM_EPS = 1e-6
ROPE_BASE = 10000.0
DA_HEAD_DIM = 128
DA_V_DIM = 2 * DA_HEAD_DIM
DA_SCALE = DA_HEAD_DIM ** -0.5
MLA_NOPE = 128
MLA_ROPE = 64
MLA_V = 128
MLA_Q_LORA = 512
MLA_KV_LORA = 512
MLA_SCALE = (MLA_NOPE + MLA_ROPE) ** -0.5
MLA_QK_PAD = 256
LOG2_E = math.log2(math.e)
DA_Q_SCALE = DA_SCALE * LOG2_E
MLA_Q_SCALE = MLA_SCALE * LOG2_E
ATTN_SUB_ROWS = 256
MXU_COLS = 256
MM_SUB_COLS = 2 * MXU_COLS
RES_SUB_ROWS = 256
Q_UP_SUB_COLS = 1024

LANES = 128
V7X_VMEM_BYTES = 64 * 1024 * 1024
VMEM_LIMIT = V7X_VMEM_BYTES - 8 * 1024 * 1024

MOD_ROWS = 16


def _params(*sem):
    return pltpu.CompilerParams(dimension_semantics=sem, vmem_limit_bytes=VMEM_LIMIT)


def _rms(x, g):
    ms = jnp.mean(x * x, axis=-1, keepdims=True)
    return x * lax.rsqrt(ms + NORM_EPS) * g


def _rot_half(x):
    return pltpu.roll(x, LANES // 2, 1)


def _mod_kernel(c_ref, w_ref, b_ref, o_ref):
    c = c_ref[...]
    s = c * jax.nn.sigmoid(c)
    o_ref[0] = jnp.dot(s.astype(BF16), w_ref[0].astype(BF16), preferred_element_type=F32) + b_ref[0]


def _mods(cond, ada_w, ada_b, tn=1024):
    depth, d, n = ada_w.shape
    return pl.pallas_call(
        _mod_kernel,
        grid=(depth, n // tn),
        in_specs=[pl.BlockSpec((MOD_ROWS, d), lambda l, j: (0, 0)),
                  pl.BlockSpec((1, d, tn), lambda l, j: (l, 0, j)),
                  pl.BlockSpec((1, 1, tn), lambda l, j: (l, 0, j))],
        out_specs=pl.BlockSpec((1, MOD_ROWS, tn), lambda l, j: (l, 0, j)),
        out_shape=jax.ShapeDtypeStruct((depth, MOD_ROWS, n), F32),
        compiler_params=_params("arbitrary", "arbitrary"),
        name="adaln_mod",
    )(cond, ada_w, ada_b.reshape(depth, 1, n))


def _norm_mod_kernel(x_ref, g_ref, shift_ref, scale_ref, o_ref):
    y = _rms(x_ref[...], g_ref[...])
    o_ref[...] = (y * (1 + scale_ref[...]) + shift_ref[...]).astype(BF16)


class _Mod(NamedTuple):
    table: jax.Array
    layer: int
    k: int
    row0: int
    groups: int


def _group_spec(m, t, tm):
    tiles_per_group = t // m.groups // tm
    d = m.table.shape[-1]
    return pl.BlockSpec((None, None, None, 1, d),
                        lambda i, *_: (m.layer, m.row0 + i // tiles_per_group, m.k, 0, 0))


def _norm_mod(x, g, shift, scale, tm=1024):
    t, d = x.shape
    return pl.pallas_call(
        _norm_mod_kernel,
        grid=(t // tm,),
        in_specs=[pl.BlockSpec((tm, d), lambda i: (i, 0)),
                  pl.BlockSpec((1, d), lambda i: (0, 0)),
                  _group_spec(shift, t, tm), _group_spec(scale, t, tm)],
        out_specs=pl.BlockSpec((tm, d), lambda i: (i, 0)),
        out_shape=jax.ShapeDtypeStruct((t, d), BF16),
        compiler_params=_params("arbitrary"),
        name="norm_mod",
    )(x, g.reshape(1, d), shift.table, scale.table)


def _qkv_rope_kernel(a_ref, w_ref, cos_ref, sin_ref, o_ref, w_bf):
    @pl.when(pl.program_id(1) == 0)
    def _():
        w_bf[...] = w_ref[...].astype(BF16)

    a = a_ref[...]
    cos = cos_ref[...]
    sin = sin_ref[...]
    for c in range(w_bf.shape[1] // MM_SUB_COLS):
        acc = jnp.dot(a, w_bf[:, c * MM_SUB_COLS:(c + 1) * MM_SUB_COLS], preferred_element_type=F32)
        for g in range(MM_SUB_COLS // LANES):
            xg = acc[:, g * LANES:(g + 1) * LANES]
            lo = c * MM_SUB_COLS + g * LANES
            o_ref[:, lo:lo + LANES] = (xg * cos + _rot_half(xg) * sin).astype(BF16)


def _qkv_rope(a, w, layer, cos_tab, sin_tab, tm=1024, tn=1024):
    t, k = a.shape
    n = w.shape[2]
    tiles_per_kind = n // 3 // tn
    row_tiles = cos_tab.shape[1] // tm
    tab_spec = pl.BlockSpec((None, tm, LANES), lambda j, i: (j // tiles_per_kind, i % row_tiles, 0))
    return pl.pallas_call(
        _qkv_rope_kernel,
        grid=(n // tn, t // tm),
        in_specs=[pl.BlockSpec((tm, k), lambda j, i: (i, 0)),
                  pl.BlockSpec((None, k, tn), lambda j, i: (layer, 0, j)),
                  tab_spec, tab_spec],
        out_specs=pl.BlockSpec((tm, tn), lambda j, i: (i, j)),
        out_shape=jax.ShapeDtypeStruct((t, n), BF16),
        scratch_shapes=[pltpu.VMEM((k, tn), BF16)],
        compiler_params=_params("arbitrary", "arbitrary"),
        name="qkv_rope",
    )(a, w, cos_tab, sin_tab)


def _swiglu_kernel(a_ref, wg_ref, wu_ref, o_ref, wg_bf, wu_bf):
    @pl.when(pl.program_id(1) == 0)
    def _():
        wg_bf[...] = wg_ref[...].astype(BF16)
        wu_bf[...] = wu_ref[...].astype(BF16)

    a = a_ref[...]
    for c in range(wg_bf.shape[1] // MXU_COLS):
        cols = slice(c * MXU_COLS, (c + 1) * MXU_COLS)
        g = jnp.dot(a, wg_bf[:, cols], preferred_element_type=F32)
        u = jnp.dot(a, wu_bf[:, cols], preferred_element_type=F32)
        o_ref[:, cols] = (g * jax.nn.sigmoid(g) * u).astype(BF16)


def _swiglu_in(a, w_in, layer, tm=1024, tn=512):
    t, k = a.shape
    hidden = w_in.shape[2] // 2
    nj = hidden // tn
    return pl.pallas_call(
        _swiglu_kernel,
        grid=(nj, t // tm),
        in_specs=[pl.BlockSpec((tm, k), lambda j, i: (i, 0)),
                  pl.BlockSpec((None, k, tn), lambda j, i: (layer, 0, j)),
                  pl.BlockSpec((None, k, tn), lambda j, i: (layer, 0, j + nj))],
        out_specs=pl.BlockSpec((tm, tn), lambda j, i: (i, j)),
        out_shape=jax.ShapeDtypeStruct((t, hidden), BF16),
        scratch_shapes=[pltpu.VMEM((k, tn), BF16), pltpu.VMEM((k, tn), BF16)],
        compiler_params=_params("arbitrary", "arbitrary"),
        name="swiglu_in",
    )(a, w_in, w_in)


def _res_kernel(out_x, next_norm, a_ref, w_ref, x_ref, gate_ref, *refs):
    refs = list(refs)
    if next_norm == "mod":
        g_ref, shift_ref, scale_ref = refs[:3]
        outs = refs[3:]
    else:
        g_ref = refs[0]
        outs = refs[1:]
    ox_ref = outs.pop(0) if out_x else None
    oh_ref = outs[0]
    for r in range(a_ref.shape[0] // RES_SUB_ROWS):
        rows = slice(r * RES_SUB_ROWS, (r + 1) * RES_SUB_ROWS)
        y = jnp.dot(a_ref[rows, :], w_ref[...], preferred_element_type=F32)
        xn = x_ref[rows, :] + gate_ref[...] * y
        if out_x:
            ox_ref[rows, :] = xn
        h = _rms(xn, g_ref[...])
        if next_norm == "mod":
            oh_ref[rows, :] = (h * (1 + scale_ref[...]) + shift_ref[...]).astype(BF16)
        else:
            oh_ref[rows, :] = h


def _res(a, w, layer, x, gate, g, shift=None, scale=None, *, out_x=True, tm):
    t, k = a.shape
    d = w.shape[2]
    next_norm = "final" if shift is None else "mod"
    row = pl.BlockSpec((tm, d), lambda i: (i, 0))
    in_specs = [pl.BlockSpec((tm, k), lambda i: (i, 0)),
                pl.BlockSpec((None, k, d), lambda i: (layer, 0, 0), pipeline_mode=pl.Buffered(1)),
                row, _group_spec(gate, t, tm), pl.BlockSpec((1, d), lambda i: (0, 0))]
    args = [a, w, x, gate.table, g.reshape(1, d)]
    if next_norm == "mod":
        in_specs += [_group_spec(shift, t, tm), _group_spec(scale, t, tm)]
        args += [shift.table, scale.table]
    out_specs, out_shape = [], []
    if out_x:
        out_specs.append(row)
        out_shape.append(jax.ShapeDtypeStruct((t, d), F32))
    out_specs.append(row)
    out_shape.append(jax.ShapeDtypeStruct((t, d), BF16 if next_norm == "mod" else F32))
    return pl.pallas_call(
        functools.partial(_res_kernel, out_x, next_norm),
        grid=(t // tm,),
        in_specs=in_specs, out_specs=out_specs, out_shape=out_shape,
        compiler_params=_params("arbitrary"),
        name="proj_residual",
    )(*args)


def _nt_dot(q, k):
    return lax.dot_general(q, k, (((1,), (1,)), ((), ())), preferred_element_type=F32)


def _softmax_pv(q, k_parts, v_refs):
    scores = [_nt_dot(q, k) for k in k_parts]
    mx = functools.reduce(jnp.maximum, [jnp.max(s, axis=1, keepdims=True) for s in scores])
    e = [jnp.exp2(s - mx) for s in scores]
    tot = functools.reduce(jnp.add, [jnp.sum(x, axis=1, keepdims=True) for x in e])
    o = functools.reduce(jnp.add, [jnp.dot(p.astype(BF16), v[...], preferred_element_type=F32)
                                   for p, v in zip(e, v_refs)])
    return o, 1.0 / tot


def _da_attn_kernel(nseg, lam_init, lamv_ref, g_ref, q_ref, *refs):
    k_refs, v_refs, o_ref = refs[:nseg], refs[nseg:2 * nseg], refs[2 * nseg]
    lv = lamv_ref[...]
    lam = (jnp.exp(jnp.sum(lv[0:1] * lv[1:2], axis=1, keepdims=True))
           - jnp.exp(jnp.sum(lv[2:3] * lv[3:4], axis=1, keepdims=True)) + lam_init)
    for hh in range(q_ref.shape[1] // DA_V_DIM):
        vcols = slice(hh * DA_V_DIM, (hh + 1) * DA_V_DIM)
        vs = [v.at[:, vcols] for v in v_refs]
        for t in range(q_ref.shape[0] // ATTN_SUB_ROWS):
            rows = slice(t * ATTN_SUB_ROWS, (t + 1) * ATTN_SUB_ROWS)
            o = None
            for m in range(2):
                sl = slice(hh * DA_V_DIM + m * DA_HEAD_DIM, hh * DA_V_DIM + (m + 1) * DA_HEAD_DIM)
                om, r = _softmax_pv(q_ref[rows, sl], [k[:, sl] for k in k_refs], vs)
                o = om * r if m == 0 else o - om * (lam * r)
            o_ref[rows, vcols] = (_rms(o, g_ref[...]) * (1.0 - lam_init)).astype(BF16)


def _da_attention(q_src, kv_srcs, lam_vecs, subln_g, lam_init, heads, tq, heads_per_step=1):
    b, sq, _ = q_src.shape
    hd = heads_per_step * DA_V_DIM
    groups = heads // heads_per_step
    in_specs = [pl.BlockSpec((4, DA_HEAD_DIM), lambda bi, h, qi: (0, 0)),
                pl.BlockSpec((1, DA_V_DIM), lambda bi, h, qi: (0, 0)),
                pl.BlockSpec((None, tq, hd), lambda bi, h, qi: (bi, qi, h))]
    args = [lam_vecs, subln_g.reshape(1, DA_V_DIM), q_src]
    for kind in (1, 2):
        for src in kv_srcs:
            in_specs.append(pl.BlockSpec((None, src.shape[1], hd),
                                         lambda bi, h, qi, kind=kind: (bi, 0, kind * groups + h)))
            args.append(src)
    return pl.pallas_call(
        functools.partial(_da_attn_kernel, len(kv_srcs), lam_init),
        grid=(b, groups, sq // tq),
        in_specs=in_specs,
        out_specs=pl.BlockSpec((None, tq, hd), lambda bi, h, qi: (bi, qi, h)),
        out_shape=jax.ShapeDtypeStruct((b, sq, heads * DA_V_DIM), BF16),
        compiler_params=_params("arbitrary", "arbitrary", "arbitrary"),
        name="diff_attention",
    )(*args)


def _mla_attn_kernel(nseg, q_ref, *refs):
    k_refs, v_refs, o_ref = refs[:nseg], refs[nseg:2 * nseg], refs[2 * nseg]
    for t in range(q_ref.shape[0] // ATTN_SUB_ROWS):
        rows = slice(t * ATTN_SUB_ROWS, (t + 1) * ATTN_SUB_ROWS)
        for hh in range(2):
            cols = slice(hh * MLA_QK_PAD, (hh + 1) * MLA_QK_PAD)
            o, r = _softmax_pv(q_ref[rows, cols], [k[:, cols] for k in k_refs], v_refs)
            half = slice(hh * MLA_V, (hh + 1) * MLA_V)
            o_ref[rows, half] = (o[:, half] * r).astype(BF16)


def _mla_attention(q, ks, vs, heads, tq):
    b, sq, _ = q.shape
    in_specs = [pl.BlockSpec((None, tq, 2 * MLA_QK_PAD), lambda bi, h, qi: (bi, qi, h))]
    in_specs += [pl.BlockSpec((None, k.shape[1], 2 * MLA_QK_PAD), lambda bi, h, qi: (bi, 0, h)) for k in ks]
    in_specs += [pl.BlockSpec((None, v.shape[1], 2 * MLA_V), lambda bi, h, qi: (bi, 0, h)) for v in vs]
    return pl.pallas_call(
        functools.partial(_mla_attn_kernel, len(ks)),
        grid=(b, heads // 2, sq // tq),
        in_specs=in_specs,
        out_specs=pl.BlockSpec((None, tq, 2 * MLA_V), lambda bi, h, qi: (bi, qi, h)),
        out_shape=jax.ShapeDtypeStruct((b, sq, heads * MLA_V), BF16),
        compiler_params=_params("arbitrary", "arbitrary", "arbitrary"),
        name="mla_attention",
    )(q, *ks, *vs)


def _mla_proj_kernel(with_q, a_ref, win_ref, qg_ref, kvg_ref, cos_ref, sin_ref, wk_ref, wv_ref, *refs):
    if with_q:
        wq_ref, q_ref, k_ref, v_ref = refs
    else:
        k_ref, v_ref = refs
    cos = cos_ref[...]
    sin = sin_ref[...]
    acc = jnp.dot(a_ref[...], win_ref[...], preferred_element_type=F32)
    ckv = _rms(acc[:, MLA_Q_LORA:MLA_Q_LORA + MLA_KV_LORA], kvg_ref[...]).astype(BF16)
    kr = acc[:, MLA_Q_LORA + MLA_KV_LORA:]
    kr = (kr * cos + _rot_half(kr) * sin).astype(BF16)
    v_ref[...] = jnp.dot(ckv, wv_ref[...], preferred_element_type=F32).astype(BF16)
    kn = jnp.dot(ckv, wk_ref[...], preferred_element_type=F32).astype(BF16)
    for h in range(kn.shape[1] // MLA_NOPE):
        base = h * MLA_QK_PAD
        k_ref[:, base:base + MLA_NOPE] = kn[:, h * MLA_NOPE:(h + 1) * MLA_NOPE]
        k_ref[:, base + MLA_NOPE:base + MLA_QK_PAD] = kr
    if with_q:
        cq = _rms(acc[:, :MLA_Q_LORA], qg_ref[...]).astype(BF16)
        qcos = cos * MLA_Q_SCALE
        qsin = sin * MLA_Q_SCALE
        for c in range(wq_ref.shape[1] // Q_UP_SUB_COLS):
            qa = jnp.dot(cq, wq_ref[:, c * Q_UP_SUB_COLS:(c + 1) * Q_UP_SUB_COLS], preferred_element_type=F32)
            for h in range(Q_UP_SUB_COLS // MLA_QK_PAD):
                lo = h * MLA_QK_PAD
                base = c * Q_UP_SUB_COLS + lo
                q_ref[:, base:base + MLA_NOPE] = (qa[:, lo:lo + MLA_NOPE] * MLA_Q_SCALE).astype(BF16)
                xr = qa[:, lo + MLA_NOPE:lo + MLA_QK_PAD]
                q_ref[:, base + MLA_NOPE:base + MLA_QK_PAD] = (xr * qcos + _rot_half(xr) * qsin).astype(BF16)


def _mla_proj(a, w_in, q_g, kv_g, cos_tab, sin_tab, wk, wv, wq=None, tm=512):
    t, k = a.shape
    with_q = wq is not None
    row_tiles = cos_tab.shape[0] // tm
    heads = wk.shape[1] // MLA_NOPE
    const = lambda arr: pl.BlockSpec(arr.shape, lambda i: (0, 0), pipeline_mode=pl.Buffered(1))
    tab = pl.BlockSpec((tm, LANES), lambda i: (i % row_tiles, 0))
    in_specs = [pl.BlockSpec((tm, k), lambda i: (i, 0)), const(w_in),
                pl.BlockSpec((1, MLA_Q_LORA), lambda i: (0, 0)),
                pl.BlockSpec((1, MLA_KV_LORA), lambda i: (0, 0)),
                tab, tab, const(wk), const(wv)]
    args = [a, w_in, q_g.reshape(1, -1), kv_g.reshape(1, -1), cos_tab, sin_tab, wk, wv]
    widths = [heads * MLA_QK_PAD, wv.shape[1]]
    if with_q:
        in_specs.append(const(wq))
        args.append(wq)
        widths.insert(0, wq.shape[1])
    return pl.pallas_call(
        functools.partial(_mla_proj_kernel, with_q),
        grid=(t // tm,),
        in_specs=in_specs,
        out_specs=[pl.BlockSpec((tm, w_), lambda i: (i, 0)) for w_ in widths],
        out_shape=[jax.ShapeDtypeStruct((t, w_), BF16) for w_ in widths],
        compiler_params=_params("arbitrary"),
        name="mla_proj",
    )(*args)


def _axial_angles(rows, rot_dim):
    row = jnp.repeat(jnp.arange(rows, dtype=F32), GRID_W)
    col = jnp.tile(jnp.arange(GRID_W, dtype=F32), rows)
    n_freq = rot_dim // 4
    inv_freq = ROPE_BASE ** (-jnp.arange(n_freq, dtype=F32) / n_freq)
    ang = jnp.concatenate([row[:, None] * inv_freq, col[:, None] * inv_freq], axis=-1)
    return jnp.cos(ang), jnp.sin(ang)


def _da_tables(seq, ctx_rows):
    cos, sin = _axial_angles(seq // GRID_W, DA_HEAD_DIM)
    cos = jnp.concatenate([cos, cos], axis=-1)
    sin = jnp.concatenate([-sin, sin], axis=-1)
    one, zero = jnp.ones_like(cos), jnp.zeros_like(cos)
    lat = (jnp.stack([cos * DA_Q_SCALE, cos, one]), jnp.stack([sin * DA_Q_SCALE, sin, zero]))
    one_c, zero_c = one[:ctx_rows], zero[:ctx_rows]
    ctx = (jnp.stack([one_c * DA_Q_SCALE, one_c, one_c]), jnp.stack([zero_c, zero_c, zero_c]))
    return lat, ctx


def _spread(x1, x2):
    z = jnp.zeros_like(x1)
    return jnp.concatenate([x1, z, x2, z], axis=-1)


def _mla_tables(seq, ctx_rows):
    cos, sin = _axial_angles(seq // GRID_W, MLA_ROPE)
    cos_l, sin_l = _spread(cos, cos), _spread(-sin, sin)
    cos_c, sin_c = jnp.ones_like(cos_l[:ctx_rows]), jnp.zeros_like(cos_l[:ctx_rows])
    return (cos_l, sin_l), (cos_c, sin_c)


def _mla_weight_layouts(w_in, w_q_up, w_kv_up):
    half = MLA_ROPE // 2
    lora = MLA_Q_LORA + MLA_KV_LORA
    w_in_p = jnp.concatenate([w_in[:, :lora], _spread(w_in[:, lora:lora + half], w_in[:, lora + half:])], axis=1)
    wq = w_q_up.reshape(MLA_Q_LORA, -1, MLA_NOPE + MLA_ROPE)
    wq_p = jnp.concatenate([wq[..., :MLA_NOPE],
                            _spread(wq[..., MLA_NOPE:MLA_NOPE + half], wq[..., MLA_NOPE + half:])], axis=-1)
    wkv = w_kv_up.reshape(MLA_KV_LORA, -1, MLA_NOPE + MLA_V)
    wk = wkv[..., :MLA_NOPE].reshape(MLA_KV_LORA, -1)
    wv = wkv[..., MLA_NOPE:].reshape(MLA_KV_LORA, -1)
    return w_in_p, wq_p.reshape(MLA_Q_LORA, -1), wk, wv


def kernel(x, c, ctx, c_ctx, ada_w, ada_b, norm_mix_g, norm_ffn_g, ffn_w_in, ffn_w_out, da_w_qkv, da_lambda, da_subln_g, da_w_o, mla_w_in, mla_q_norm_g, mla_w_q_up, mla_kv_norm_g, mla_w_kv_up, mla_w_o, final_norm_g):
    b, s, d = x.shape
    cl = ctx.shape[1]
    assert ada_w.shape[0] == 2 and b + 1 <= MOD_ROWS
    da_heads = d // DA_V_DIM
    mla_heads = d // MLA_V

    cond = jnp.concatenate([c, c_ctx[None], jnp.zeros((MOD_ROWS - b - 1, d), F32)], axis=0)
    mods = _mods(cond, ada_w, ada_b).reshape(ada_w.shape[0], MOD_ROWS, 6, 1, d)

    def lat_mod(i, k):
        return _Mod(mods, i, k, 0, b)

    def ctx_mod(i, k):
        return _Mod(mods, i, k, b, 1)

    bf = lambda w: w.astype(BF16)
    w_o0, w_o1, w_ffn_out = bf(da_w_o), bf(mla_w_o), bf(ffn_w_out)
    w_in_p, wq_p, wk, wv = (bf(w) for w in _mla_weight_layouts(mla_w_in[0], mla_w_q_up[0], mla_w_kv_up[0]))
    da_lat_tab, da_ctx_tab = _da_tables(s, 1024)
    mla_lat_tab, mla_ctx_tab = _mla_tables(s, 512)

    xl = x.reshape(b * s, d)
    xc = ctx.reshape(b * cl, d)

    lam_init = 0.8 - 0.6 * math.exp(-0.3 * 0)
    h_l = _norm_mod(xl, norm_mix_g[0], lat_mod(0, 0), lat_mod(0, 1))
    h_c = _norm_mod(xc, norm_mix_g[0], ctx_mod(0, 0), ctx_mod(0, 1))
    qkv_l = _qkv_rope(h_l, da_w_qkv, 0, *da_lat_tab).reshape(b, s, -1)
    qkv_c = _qkv_rope(h_c, da_w_qkv, 0, *da_ctx_tab).reshape(b, cl, -1)
    a_l = _da_attention(qkv_l, [qkv_c, qkv_l], da_lambda[0], da_subln_g[0], lam_init, da_heads, tq=s)
    a_c = _da_attention(qkv_c, [qkv_c], da_lambda[0], da_subln_g[0], lam_init, da_heads, tq=cl,
                        heads_per_step=da_heads)
    xl, hf_l = _res(a_l.reshape(b * s, d), w_o0, 0, xl, lat_mod(0, 2), norm_ffn_g[0], lat_mod(0, 3), lat_mod(0, 4), tm=512)
    xc, hf_c = _res(a_c.reshape(b * cl, d), w_o0, 0, xc, ctx_mod(0, 2), norm_ffn_g[0], ctx_mod(0, 3), ctx_mod(0, 4), tm=512)
    act_l = _swiglu_in(hf_l, ffn_w_in, 0, tm=2048)
    act_c = _swiglu_in(hf_c, ffn_w_in, 0, tm=2048)
    xl, hm_l = _res(act_l, w_ffn_out, 0, xl, lat_mod(0, 5), norm_mix_g[1], lat_mod(1, 0), lat_mod(1, 1), tm=256)
    (hm_c,) = _res(act_c, w_ffn_out, 0, xc, ctx_mod(0, 5), norm_mix_g[1], ctx_mod(1, 0), ctx_mod(1, 1),
                   out_x=False, tm=256)

    q_l, k_l, v_l = _mla_proj(hm_l, w_in_p, mla_q_norm_g[0], mla_kv_norm_g[0], *mla_lat_tab, wk, wv, wq_p)
    k_c, v_c = _mla_proj(hm_c, w_in_p, mla_q_norm_g[0], mla_kv_norm_g[0], *mla_ctx_tab, wk, wv)
    a_l = _mla_attention(q_l.reshape(b, s, -1),
                         [k_c.reshape(b, cl, -1), k_l.reshape(b, s, -1)],
                         [v_c.reshape(b, cl, -1), v_l.reshape(b, s, -1)], mla_heads, tq=s)
    xl, hf_l = _res(a_l.reshape(b * s, d), w_o1, 0, xl, lat_mod(1, 2), norm_ffn_g[1], lat_mod(1, 3), lat_mod(1, 4), tm=512)
    act_l = _swiglu_in(hf_l, ffn_w_in, 1, tm=2048)
    (out,) = _res(act_l, w_ffn_out, 1, xl, lat_mod(1, 5), final_norm_g, out_x=False, tm=256)
    return out.reshape(b, s, d)
```

```python
import functools
import math
from typing import NamedTuple

import jax
import jax.numpy as jnp
from jax import lax
from jax.experimental import pallas as pl
from jax.experimental.pallas import tpu as pltpu

F32 = jnp.float32
BF16 = jnp.bfloat16

GRID_W = 64
NORM_EPS = 1e-6
ROPE_BASE = 10000.0
DA_HEAD_DIM = 128
DA_V_DIM = 2 * DA_HEAD_DIM
DA_SCALE = DA_HEAD_DIM ** -0.5
MLA_NOPE = 128
MLA_ROPE = 64
MLA_V = 128
MLA_Q_LORA = 512
MLA_KV_LORA = 512
MLA_SCALE = (MLA_NOPE + MLA_ROPE) ** -0.5
MLA_QK_PAD = 256
LOG2_E = math.log2(math.e)
DA_Q_SCALE = DA_SCALE * LOG2_E
MLA_Q_SCALE = MLA_SCALE * LOG2_E
ATTN_SUB_ROWS = 256
MM_SUB_ROWS = 256
MM_SUB_COLS = 256
RES_SUB_ROWS = 256
Q_UP_SUB_COLS = 1024

LANES = 128
V7X_VMEM_BYTES = 64 * 1024 * 1024
VMEM_LIMIT = V7X_VMEM_BYTES - 8 * 1024 * 1024

MOD_ROWS = 16


def _params(*sem):
    return pltpu.CompilerParams(dimension_semantics=sem, vmem_limit_bytes=VMEM_LIMIT)


def _rms(x, g):
    ms = jnp.mean(x * x, axis=-1, keepdims=True)
    return x * lax.rsqrt(ms + NORM_EPS) * g


def _rot_half(x):
    return pltpu.roll(x, LANES // 2, 1)


def _mod_kernel(c_ref, w_ref, b_ref, o_ref):
    c = c_ref[...]
    s = c * jax.nn.sigmoid(c)
    o_ref[0] = jnp.dot(s.astype(BF16), w_ref[0].astype(BF16), preferred_element_type=F32) + b_ref[0]


def _mods(cond, ada_w, ada_b, tn=1024):
    depth, d, n = ada_w.shape
    return pl.pallas_call(
        _mod_kernel,
        grid=(depth, n // tn),
        in_specs=[pl.BlockSpec((MOD_ROWS, d), lambda l, j: (0, 0)),
                  pl.BlockSpec((1, d, tn), lambda l, j: (l, 0, j)),
                  pl.BlockSpec((1, 1, tn), lambda l, j: (l, 0, j))],
        out_specs=pl.BlockSpec((1, MOD_ROWS, tn), lambda l, j: (l, 0, j)),
        out_shape=jax.ShapeDtypeStruct((depth, MOD_ROWS, n), F32),
        compiler_params=_params("arbitrary", "arbitrary"),
        name="adaln_mod",
    )(cond, ada_w, ada_b.reshape(depth, 1, n))


def _norm_mod_kernel(x_ref, g_ref, shift_ref, scale_ref, o_ref):
    y = _rms(x_ref[...], g_ref[...])
    o_ref[...] = (y * (1 + scale_ref[...]) + shift_ref[...]).astype(BF16)


class _Mod(NamedTuple):
    table: jax.Array
    layer: int
    k: int
    row0: int
    groups: int


def _group_spec(m, t, tm):
    tiles_per_group = t // m.groups // tm
    d = m.table.shape[-1]
    return pl.BlockSpec((None, None, None, 1, d),
                        lambda i, *_: (m.layer, m.row0 + i // tiles_per_group, m.k, 0, 0))


def _norm_mod(x, g, shift, scale, tm=1024):
    t, d = x.shape
    return pl.pallas_call(
        _norm_mod_kernel,
        grid=(t // tm,),
        in_specs=[pl.BlockSpec((tm, d), lambda i: (i, 0)),
                  pl.BlockSpec((1, d), lambda i: (0, 0)),
                  _group_spec(shift, t, tm), _group_spec(scale, t, tm)],
        out_specs=pl.BlockSpec((tm, d), lambda i: (i, 0)),
        out_shape=jax.ShapeDtypeStruct((t, d), BF16),
        compiler_params=_params("arbitrary"),
        name="norm_mod",
    )(x, g.reshape(1, d), shift.table, scale.table)


def _qkv_rope_kernel(a_ref, w_ref, cos_ref, sin_ref, o_ref, w_bf):
    @pl.when(pl.program_id(1) == 0)
    def _():
        w_bf[...] = w_ref[...].astype(BF16)

    for r in range(a_ref.shape[0] // MM_SUB_ROWS):
        rows = slice(r * MM_SUB_ROWS, (r + 1) * MM_SUB_ROWS)
        acc = jnp.dot(a_ref[rows, :], w_bf[...], preferred_element_type=F32)
        cos = cos_ref[rows, :]
        sin = sin_ref[rows, :]
        for g in range(acc.shape[1] // LANES):
            sl = slice(g * LANES, (g + 1) * LANES)
            xg = acc[:, sl]
            o_ref[rows, sl] = (xg * cos + _rot_half(xg) * sin).astype(BF16)


def _qkv_rope(a, w, layer, cos_tab, sin_tab, tm=1024, tn=1024):
    t, k = a.shape
    n = w.shape[2]
    tiles_per_kind = n // 3 // tn
    row_tiles = cos_tab.shape[1] // tm
    tab_spec = pl.BlockSpec((None, tm, LANES), lambda j, i: (j // tiles_per_kind, i % row_tiles, 0))
    return pl.pallas_call(
        _qkv_rope_kernel,
        grid=(n // tn, t // tm),
        in_specs=[pl.BlockSpec((tm, k), lambda j, i: (i, 0)),
                  pl.BlockSpec((None, k, tn), lambda j, i: (layer, 0, j)),
                  tab_spec, tab_spec],
        out_specs=pl.BlockSpec((tm, tn), lambda j, i: (i, j)),
        out_shape=jax.ShapeDtypeStruct((t, n), BF16),
        scratch_shapes=[pltpu.VMEM((k, tn), BF16)],
        compiler_params=_params("arbitrary", "arbitrary"),
        name="qkv_rope",
    )(a, w, cos_tab, sin_tab)


def _swiglu_kernel(a_ref, wg_ref, wu_ref, o_ref, wg_bf, wu_bf):
    @pl.when(pl.program_id(1) == 0)
    def _():
        wg_bf[...] = wg_ref[...].astype(BF16)
        wu_bf[...] = wu_ref[...].astype(BF16)

    for r in range(a_ref.shape[0] // MM_SUB_ROWS):
        rows = slice(r * MM_SUB_ROWS, (r + 1) * MM_SUB_ROWS)
        a = a_ref[rows, :]
        for c in range(wg_bf.shape[1] // MM_SUB_COLS):
            cols = slice(c * MM_SUB_COLS, (c + 1) * MM_SUB_COLS)
            g = jnp.dot(a, wg_bf[:, cols], preferred_element_type=F32)
            u = jnp.dot(a, wu_bf[:, cols], preferred_element_type=F32)
            o_ref[rows, cols] = (g * jax.nn.sigmoid(g) * u).astype(BF16)


def _swiglu_in(a, w_in, layer, tm=2048, tn=512):
    t, k = a.shape
    hidden = w_in.shape[2] // 2
    nj = hidden // tn
    return pl.pallas_call(
        _swiglu_kernel,
        grid=(nj, t // tm),
        in_specs=[pl.BlockSpec((tm, k), lambda j, i: (i, 0)),
                  pl.BlockSpec((None, k, tn), lambda j, i: (layer, 0, j)),
                  pl.BlockSpec((None, k, tn), lambda j, i: (layer, 0, j + nj))],
        out_specs=pl.BlockSpec((tm, tn), lambda j, i: (i, j)),
        out_shape=jax.ShapeDtypeStruct((t, hidden), BF16),
        scratch_shapes=[pltpu.VMEM((k, tn), BF16), pltpu.VMEM((k, tn), BF16)],
        compiler_params=_params("arbitrary", "arbitrary"),
        name="swiglu_in",
    )(a, w_in, w_in)


def _res_kernel(out_x, next_norm, a_ref, w_ref, x_ref, gate_ref, *refs):
    refs = list(refs)
    if next_norm == "mod":
        g_ref, shift_ref, scale_ref = refs[:3]
        outs = refs[3:]
    else:
        g_ref = refs[0]
        outs = refs[1:]
    ox_ref = outs.pop(0) if out_x else None
    oh_ref = outs[0]
    for r in range(a_ref.shape[0] // RES_SUB_ROWS):
        rows = slice(r * RES_SUB_ROWS, (r + 1) * RES_SUB_ROWS)
        y = jnp.dot(a_ref[rows, :], w_ref[...], preferred_element_type=F32)
        xn = x_ref[rows, :] + gate_ref[...] * y
        if out_x:
            ox_ref[rows, :] = xn
        h = _rms(xn, g_ref[...])
        if next_norm == "mod":
            oh_ref[rows, :] = (h * (1 + scale_ref[...]) + shift_ref[...]).astype(BF16)
        else:
            oh_ref[rows, :] = h


def _res(a, w, layer, x, gate, g, shift=None, scale=None, *, out_x=True, tm):
    t, k = a.shape
    d = w.shape[2]
    next_norm = "final" if shift is None else "mod"
    row = pl.BlockSpec((tm, d), lambda i: (i, 0))
    in_specs = [pl.BlockSpec((tm, k), lambda i: (i, 0)),
                pl.BlockSpec((None, k, d), lambda i: (layer, 0, 0), pipeline_mode=pl.Buffered(1)),
                row, _group_spec(gate, t, tm), pl.BlockSpec((1, d), lambda i: (0, 0))]
    args = [a, w, x, gate.table, g.reshape(1, d)]
    if next_norm == "mod":
        in_specs += [_group_spec(shift, t, tm), _group_spec(scale, t, tm)]
        args += [shift.table, scale.table]
    out_specs, out_shape = [], []
    if out_x:
        out_specs.append(row)
        out_shape.append(jax.ShapeDtypeStruct((t, d), F32))
    out_specs.append(row)
    out_shape.append(jax.ShapeDtypeStruct((t, d), BF16 if next_norm == "mod" else F32))
    return pl.pallas_call(
        functools.partial(_res_kernel, out_x, next_norm),
        grid=(t // tm,),
        in_specs=in_specs, out_specs=out_specs, out_shape=out_shape,
        compiler_params=_params("arbitrary"),
        name="proj_residual",
    )(*args)


def _nt_dot(q, k):
    return lax.dot_general(q, k, (((1,), (1,)), ((), ())), preferred_element_type=F32)


def _softmax_pv(q, k_parts, v_refs):
    scores = [_nt_dot(q, k) for k in k_parts]
    mx = functools.reduce(jnp.maximum, [jnp.max(s, axis=1, keepdims=True) for s in scores])
    e = [jnp.exp2(s - mx) for s in scores]
    tot = functools.reduce(jnp.add, [jnp.sum(x, axis=1, keepdims=True) for x in e])
    o = functools.reduce(jnp.add, [jnp.dot(p.astype(BF16), v[...], preferred_element_type=F32)
                                   for p, v in zip(e, v_refs)])
    return o, 1.0 / tot


def _da_attn_kernel(nseg, lam_init, lamv_ref, g_ref, q_ref, *refs):
    k_refs, v_refs, o_ref = refs[:nseg], refs[nseg:2 * nseg], refs[2 * nseg]
    lv = lamv_ref[...]
    lam = (jnp.exp(jnp.sum(lv[0:1] * lv[1:2], axis=1, keepdims=True))
           - jnp.exp(jnp.sum(lv[2:3] * lv[3:4], axis=1, keepdims=True)) + lam_init)
    for hh in range(q_ref.shape[1] // DA_V_DIM):
        vcols = slice(hh * DA_V_DIM, (hh + 1) * DA_V_DIM)
        vs = [v.at[:, vcols] for v in v_refs]
        for t in range(q_ref.shape[0] // ATTN_SUB_ROWS):
            rows = slice(t * ATTN_SUB_ROWS, (t + 1) * ATTN_SUB_ROWS)
            o = None
            for m in range(2):
                sl = slice(hh * DA_V_DIM + m * DA_HEAD_DIM, hh * DA_V_DIM + (m + 1) * DA_HEAD_DIM)
                om, r = _softmax_pv(q_ref[rows, sl], [k[:, sl] for k in k_refs], vs)
                o = om * r if m == 0 else o - om * (lam * r)
            o_ref[rows, vcols] = (_rms(o, g_ref[...]) * (1.0 - lam_init)).astype(BF16)


def _da_attention(q_src, kv_srcs, lam_vecs, subln_g, lam_init, heads, tq, heads_per_step=1):
    b, sq, _ = q_src.shape
    hd = heads_per_step * DA_V_DIM
    groups = heads // heads_per_step
    in_specs = [pl.BlockSpec((4, DA_HEAD_DIM), lambda bi, h, qi: (0, 0)),
                pl.BlockSpec((1, DA_V_DIM), lambda bi, h, qi: (0, 0)),
                pl.BlockSpec((None, tq, hd), lambda bi, h, qi: (bi, qi, h))]
    args = [lam_vecs, subln_g.reshape(1, DA_V_DIM), q_src]
    for kind in (1, 2):
        for src in kv_srcs:
            in_specs.append(pl.BlockSpec((None, src.shape[1], hd),
                                         lambda bi, h, qi, kind=kind: (bi, 0, kind * groups + h)))
            args.append(src)
    return pl.pallas_call(
        functools.partial(_da_attn_kernel, len(kv_srcs), lam_init),
        grid=(b, groups, sq // tq),
        in_specs=in_specs,
        out_specs=pl.BlockSpec((None, tq, hd), lambda bi, h, qi: (bi, qi, h)),
        out_shape=jax.ShapeDtypeStruct((b, sq, heads * DA_V_DIM), BF16),
        compiler_params=_params("arbitrary", "arbitrary", "arbitrary"),
        name="diff_attention",
    )(*args)


def _mla_attn_kernel(nseg, q_ref, *refs):
    k_refs, v_refs, o_ref = refs[:nseg], refs[nseg:2 * nseg], refs[2 * nseg]
    for t in range(q_ref.shape[0] // ATTN_SUB_ROWS):
        rows = slice(t * ATTN_SUB_ROWS, (t + 1) * ATTN_SUB_ROWS)
        for hh in range(2):
            cols = slice(hh * MLA_QK_PAD, (hh + 1) * MLA_QK_PAD)
            o, r = _softmax_pv(q_ref[rows, cols], [k[:, cols] for k in k_refs], v_refs)
            half = slice(hh * MLA_V, (hh + 1) * MLA_V)
            o_ref[rows, half] = (o[:, half] * r).astype(BF16)


def _mla_attention(q, ks, vs, heads, tq):
    b, sq, _ = q.shape
    in_specs = [pl.BlockSpec((None, tq, 2 * MLA_QK_PAD), lambda bi, h, qi: (bi, qi, h))]
    in_specs += [pl.BlockSpec((None, k.shape[1], 2 * MLA_QK_PAD), lambda bi, h, qi: (bi, 0, h)) for k in ks]
    in_specs += [pl.BlockSpec((None, v.shape[1], 2 * MLA_V), lambda bi, h, qi: (bi, 0, h)) for v in vs]
    return pl.pallas_call(
        functools.partial(_mla_attn_kernel, len(ks)),
        grid=(b, heads // 2, sq // tq),
        in_specs=in_specs,
        out_specs=pl.BlockSpec((None, tq, 2 * MLA_V), lambda bi, h, qi: (bi, qi, h)),
        out_shape=jax.ShapeDtypeStruct((b, sq, heads * MLA_V), BF16),
        compiler_params=_params("arbitrary", "arbitrary", "arbitrary"),
        name="mla_attention",
    )(q, *ks, *vs)


def _mla_proj_kernel(with_q, a_ref, win_ref, qg_ref, kvg_ref, cos_ref, sin_ref, wk_ref, wv_ref, *refs):
    if with_q:
        wq_ref, q_ref, k_ref, v_ref = refs
    else:
        k_ref, v_ref = refs
    cos = cos_ref[...]
    sin = sin_ref[...]
    acc = jnp.dot(a_ref[...], win_ref[...], preferred_element_type=F32)
    ckv = _rms(acc[:, MLA_Q_LORA:MLA_Q_LORA + MLA_KV_LORA], kvg_ref[...]).astype(BF16)
    kr = acc[:, MLA_Q_LORA + MLA_KV_LORA:]
    kr = (kr * cos + _rot_half(kr) * sin).astype(BF16)
    v_ref[...] = jnp.dot(ckv, wv_ref[...], preferred_element_type=F32).astype(BF16)
    kn = jnp.dot(ckv, wk_ref[...], preferred_element_type=F32).astype(BF16)
    for h in range(kn.shape[1] // MLA_NOPE):
        base = h * MLA_QK_PAD
        k_ref[:, base:base + MLA_NOPE] = kn[:, h * MLA_NOPE:(h + 1) * MLA_NOPE]
        k_ref[:, base + MLA_NOPE:base + MLA_QK_PAD] = kr
    if with_q:
        cq = _rms(acc[:, :MLA_Q_LORA], qg_ref[...]).astype(BF16)
        qcos = cos * MLA_Q_SCALE
        qsin = sin * MLA_Q_SCALE
        for c in range(wq_ref.shape[1] // Q_UP_SUB_COLS):
            qa = jnp.dot(cq, wq_ref[:, c * Q_UP_SUB_COLS:(c + 1) * Q_UP_SUB_COLS], preferred_element_type=F32)
            for h in range(Q_UP_SUB_COLS // MLA_QK_PAD):
                lo = h * MLA_QK_PAD
                base = c * Q_UP_SUB_COLS + lo
                q_ref[:, base:base + MLA_NOPE] = (qa[:, lo:lo + MLA_NOPE] * MLA_Q_SCALE).astype(BF16)
                xr = qa[:, lo + MLA_NOPE:lo + MLA_QK_PAD]
                q_ref[:, base + MLA_NOPE:base + MLA_QK_PAD] = (xr * qcos + _rot_half(xr) * qsin).astype(BF16)


def _mla_proj(a, w_in, q_g, kv_g, cos_tab, sin_tab, wk, wv, wq=None, tm=512):
    t, k = a.shape
    with_q = wq is not None
    row_tiles = cos_tab.shape[0] // tm
    heads = wk.shape[1] // MLA_NOPE
    const = lambda arr: pl.BlockSpec(arr.shape, lambda i: (0, 0), pipeline_mode=pl.Buffered(1))
    tab = pl.BlockSpec((tm, LANES), lambda i: (i % row_tiles, 0))
    in_specs = [pl.BlockSpec((tm, k), lambda i: (i, 0)), const(w_in),
                pl.BlockSpec((1, MLA_Q_LORA), lambda i: (0, 0)),
                pl.BlockSpec((1, MLA_KV_LORA), lambda i: (0, 0)),
                tab, tab, const(wk), const(wv)]
    args = [a, w_in, q_g.reshape(1, -1), kv_g.reshape(1, -1), cos_tab, sin_tab, wk, wv]
    widths = [heads * MLA_QK_PAD, wv.shape[1]]
    if with_q:
        in_specs.append(const(wq))
        args.append(wq)
        widths.insert(0, wq.shape[1])
    return pl.pallas_call(
        functools.partial(_mla_proj_kernel, with_q),
        grid=(t // tm,),
        in_specs=in_specs,
        out_specs=[pl.BlockSpec((tm, w_), lambda i: (i, 0)) for w_ in widths],
        out_shape=[jax.ShapeDtypeStruct((t, w_), BF16) for w_ in widths],
        compiler_params=_params("arbitrary"),
        name="mla_proj",
    )(*args)


def _axial_angles(rows, rot_dim):
    row = jnp.repeat(jnp.arange(rows, dtype=F32), GRID_W)
    col = jnp.tile(jnp.arange(GRID_W, dtype=F32), rows)
    n_freq = rot_dim // 4
    inv_freq = ROPE_BASE ** (-jnp.arange(n_freq, dtype=F32) / n_freq)
    ang = jnp.concatenate([row[:, None] * inv_freq, col[:, None] * inv_freq], axis=-1)
    return jnp.cos(ang), jnp.sin(ang)


def _da_tables(seq, ctx_rows):
    cos, sin = _axial_angles(seq // GRID_W, DA_HEAD_DIM)
    cos = jnp.concatenate([cos, cos], axis=-1)
    sin = jnp.concatenate([-sin, sin], axis=-1)
    one, zero = jnp.ones_like(cos), jnp.zeros_like(cos)
    lat = (jnp.stack([cos * DA_Q_SCALE, cos, one]), jnp.stack([sin * DA_Q_SCALE, sin, zero]))
    one_c, zero_c = one[:ctx_rows], zero[:ctx_rows]
    ctx = (jnp.stack([one_c * DA_Q_SCALE, one_c, one_c]), jnp.stack([zero_c, zero_c, zero_c]))
    return lat, ctx


def _spread(x1, x2):
    z = jnp.zeros_like(x1)
    return jnp.concatenate([x1, z, x2, z], axis=-1)


def _mla_tables(seq, ctx_rows):
    cos, sin = _axial_angles(seq // GRID_W, MLA_ROPE)
    cos_l, sin_l = _spread(cos, cos), _spread(-sin, sin)
    cos_c, sin_c = jnp.ones_like(cos_l[:ctx_rows]), jnp.zeros_like(cos_l[:ctx_rows])
    return (cos_l, sin_l), (cos_c, sin_c)


def _mla_weight_layouts(w_in, w_q_up, w_kv_up):
    half = MLA_ROPE // 2
    lora = MLA_Q_LORA + MLA_KV_LORA
    w_in_p = jnp.concatenate([w_in[:, :lora], _spread(w_in[:, lora:lora + half], w_in[:, lora + half:])], axis=1)
    wq = w_q_up.reshape(MLA_Q_LORA, -1, MLA_NOPE + MLA_ROPE)
    wq_p = jnp.concatenate([wq[..., :MLA_NOPE],
                            _spread(wq[..., MLA_NOPE:MLA_NOPE + half], wq[..., MLA_NOPE + half:])], axis=-1)
    wkv = w_kv_up.reshape(MLA_KV_LORA, -1, MLA_NOPE + MLA_V)
    wk = wkv[..., :MLA_NOPE].reshape(MLA_KV_LORA, -1)
    wv = wkv[..., MLA_NOPE:].reshape(MLA_KV_LORA, -1)
    return w_in_p, wq_p.reshape(MLA_Q_LORA, -1), wk, wv


def kernel(x, c, ctx, c_ctx, ada_w, ada_b, norm_mix_g, norm_ffn_g, ffn_w_in, ffn_w_out, da_w_qkv, da_lambda, da_subln_g, da_w_o, mla_w_in, mla_q_norm_g, mla_w_q_up, mla_kv_norm_g, mla_w_kv_up, mla_w_o, final_norm_g):
    b, s, d = x.shape
    cl = ctx.shape[1]
    assert ada_w.shape[0] == 2 and b + 1 <= MOD_ROWS
    da_heads = d // DA_V_DIM
    mla_heads = d // MLA_V

    cond = jnp.concatenate([c, c_ctx[None], jnp.zeros((MOD_ROWS - b - 1, d), F32)], axis=0)
    mods = _mods(cond, ada_w, ada_b).reshape(ada_w.shape[0], MOD_ROWS, 6, 1, d)

    def lat_mod(i, k):
        return _Mod(mods, i, k, 0, b)

    def ctx_mod(i, k):
        return _Mod(mods, i, k, b, 1)

    bf = lambda w: w.astype(BF16)
    w_o0, w_o1, w_ffn_out = bf(da_w_o), bf(mla_w_o), bf(ffn_w_out)
    w_in_p, wq_p, wk, wv = (bf(w) for w in _mla_weight_layouts(mla_w_in[0], mla_w_q_up[0], mla_w_kv_up[0]))
    da_lat_tab, da_ctx_tab = _da_tables(s, 1024)
    mla_lat_tab, mla_ctx_tab = _mla_tables(s, 512)

    xl = x.reshape(b * s, d)
    xc = ctx.reshape(b * cl, d)

    lam_init = 0.8 - 0.6 * math.exp(-0.3 * 0)
    h_l = _norm_mod(xl, norm_mix_g[0], lat_mod(0, 0), lat_mod(0, 1))
    h_c = _norm_mod(xc, norm_mix_g[0], ctx_mod(0, 0), ctx_mod(0, 1))
    qkv_l = _qkv_rope(h_l, da_w_qkv, 0, *da_lat_tab).reshape(b, s, -1)
    qkv_c = _qkv_rope(h_c, da_w_qkv, 0, *da_ctx_tab).reshape(b, cl, -1)
    a_l = _da_attention(qkv_l, [qkv_c, qkv_l], da_lambda[0], da_subln_g[0], lam_init, da_heads, tq=s)
    a_c = _da_attention(qkv_c, [qkv_c], da_lambda[0], da_subln_g[0], lam_init, da_heads, tq=cl,
                        heads_per_step=da_heads)
    xl, hf_l = _res(a_l.reshape(b * s, d), w_o0, 0, xl, lat_mod(0, 2), norm_ffn_g[0], lat_mod(0, 3), lat_mod(0, 4), tm=512)
    xc, hf_c = _res(a_c.reshape(b * cl, d), w_o0, 0, xc, ctx_mod(0, 2), norm_ffn_g[0], ctx_mod(0, 3), ctx_mod(0, 4), tm=512)
    act_l = _swiglu_in(hf_l, ffn_w_in, 0)
    act_c = _swiglu_in(hf_c, ffn_w_in, 0)
    xl, hm_l = _res(act_l, w_ffn_out, 0, xl, lat_mod(0, 5), norm_mix_g[1], lat_mod(1, 0), lat_mod(1, 1), tm=256)
    (hm_c,) = _res(act_c, w_ffn_out, 0, xc, ctx_mod(0, 5), norm_mix_g[1], ctx_mod(1, 0), ctx_mod(1, 1),
                   out_x=False, tm=512)

    q_l, k_l, v_l = _mla_proj(hm_l, w_in_p, mla_q_norm_g[0], mla_kv_norm_g[0], *mla_lat_tab, wk, wv, wq_p)
    k_c, v_c = _mla_proj(hm_c, w_in_p, mla_q_norm_g[0], mla_kv_norm_g[0], *mla_ctx_tab, wk, wv)
    a_l = _mla_attention(q_l.reshape(b, s, -1),
                         [k_c.reshape(b, cl, -1), k_l.reshape(b, s, -1)],
                         [v_c.reshape(b, cl, -1), v_l.reshape(b, s, -1)], mla_heads, tq=s)
    xl, hf_l = _res(a_l.reshape(b * s, d), w_o1, 0, xl, lat_mod(1, 2), norm_ffn_g[1], lat_mod(1, 3), lat_mod(1, 4), tm=512)
    act_l = _swiglu_in(hf_l, ffn_w_in, 1)
    (out,) = _res(act_l, w_ffn_out, 1, xl, lat_mod(1, 5), final_norm_g, out_x=False, tm=512)
    return out.reshape(b, s, d)
```

```python
import functools
import math
from typing import NamedTuple

import jax
import jax.numpy as jnp
from jax import lax
from jax.experimental import pallas as pl
from jax.experimental.pallas import tpu as pltpu

F32 = jnp.float32
BF16 = jnp.bfloat16

GRID_W = 64
NORM_EPS = 1e-6
ROPE_BASE = 10000.0
DA_HEAD_DIM = 128
DA_V_DIM = 2 * DA_HEAD_DIM
DA_SCALE = DA_HEAD_DIM ** -0.5
MLA_NOPE = 128
MLA_ROPE = 64
MLA_V = 128
MLA_Q_LORA = 512
MLA_KV_LORA = 512
MLA_SCALE = (MLA_NOPE + MLA_ROPE) ** -0.5
MLA_QK_PAD = 256
LOG2_E = math.log2(math.e)
DA_Q_SCALE = DA_SCALE * LOG2_E
MLA_Q_SCALE = MLA_SCALE * LOG2_E
ATTN_SUB_ROWS = 256
MM_SUB_ROWS = 256
MM_SUB_COLS = 256
RES_SUB_ROWS = 256
Q_UP_SUB_COLS = 1024

LANES = 128
V7X_VMEM_BYTES = 64 * 1024 * 1024
VMEM_LIMIT = V7X_VMEM_BYTES - 8 * 1024 * 1024

MOD_ROWS = 16


def _params(*sem):
    return pltpu.CompilerParams(dimension_semantics=sem, vmem_limit_bytes=VMEM_LIMIT)


def _rms(x, g):
    ms = jnp.mean(x * x, axis=-1, keepdims=True)
    return x * lax.rsqrt(ms + NORM_EPS) * g


def _rot_half(x):
    return pltpu.roll(x, LANES // 2, 1)


def _mod_kernel(c_ref, w_ref, b_ref, o_ref):
    c = c_ref[...]
    s = c * jax.nn.sigmoid(c)
    o_ref[0] = jnp.dot(s.astype(BF16), w_ref[0].astype(BF16), preferred_element_type=F32) + b_ref[0]


def _mods(cond, ada_w, ada_b, tn=2048):
    depth, d, n = ada_w.shape
    return pl.pallas_call(
        _mod_kernel,
        grid=(depth, n // tn),
        in_specs=[pl.BlockSpec((MOD_ROWS, d), lambda l, j: (0, 0)),
                  pl.BlockSpec((1, d, tn), lambda l, j: (l, 0, j)),
                  pl.BlockSpec((1, 1, tn), lambda l, j: (l, 0, j))],
        out_specs=pl.BlockSpec((1, MOD_ROWS, tn), lambda l, j: (l, 0, j)),
        out_shape=jax.ShapeDtypeStruct((depth, MOD_ROWS, n), F32),
        compiler_params=_params("arbitrary", "arbitrary"),
        name="adaln_mod",
    )(cond, ada_w, ada_b.reshape(depth, 1, n))


def _norm_mod_kernel(x_ref, g_ref, shift_ref, scale_ref, o_ref):
    y = _rms(x_ref[...], g_ref[...])
    o_ref[...] = (y * (1 + scale_ref[...]) + shift_ref[...]).astype(BF16)


class _Mod(NamedTuple):
    table: jax.Array
    layer: int
    k: int
    row0: int
    groups: int


def _group_spec(m, t, tm):
    tiles_per_group = t // m.groups // tm
    d = m.table.shape[-1]
    return pl.BlockSpec((None, None, None, 1, d),
                        lambda i, *_: (m.layer, m.row0 + i // tiles_per_group, m.k, 0, 0))


def _norm_mod(x, g, shift, scale, tm=1024):
    t, d = x.shape
    return pl.pallas_call(
        _norm_mod_kernel,
        grid=(t // tm,),
        in_specs=[pl.BlockSpec((tm, d), lambda i: (i, 0)),
                  pl.BlockSpec((1, d), lambda i: (0, 0)),
                  _group_spec(shift, t, tm), _group_spec(scale, t, tm)],
        out_specs=pl.BlockSpec((tm, d), lambda i: (i, 0)),
        out_shape=jax.ShapeDtypeStruct((t, d), BF16),
        compiler_params=_params("arbitrary"),
        name="norm_mod",
    )(x, g.reshape(1, d), shift.table, scale.table)


def _qkv_rope_kernel(a_ref, w_ref, cos_ref, sin_ref, o_ref, w_bf):
    @pl.when(pl.program_id(1) == 0)
    def _():
        w_bf[...] = w_ref[...].astype(BF16)

    for r in range(a_ref.shape[0] // MM_SUB_ROWS):
        rows = slice(r * MM_SUB_ROWS, (r + 1) * MM_SUB_ROWS)
        acc = jnp.dot(a_ref[rows, :], w_bf[...], preferred_element_type=F32)
        cos = cos_ref[rows, :]
        sin = sin_ref[rows, :]
        for g in range(acc.shape[1] // LANES):
            sl = slice(g * LANES, (g + 1) * LANES)
            xg = acc[:, sl]
            o_ref[rows, sl] = (xg * cos + _rot_half(xg) * sin).astype(BF16)


def _qkv_rope(a, w, layer, cos_tab, sin_tab, tm=1024, tn=1024):
    t, k = a.shape
    n = w.shape[2]
    tiles_per_kind = n // 3 // tn
    row_tiles = cos_tab.shape[1] // tm
    tab_spec = pl.BlockSpec((None, tm, LANES), lambda j, i: (j // tiles_per_kind, i % row_tiles, 0))
    return pl.pallas_call(
        _qkv_rope_kernel,
        grid=(n // tn, t // tm),
        in_specs=[pl.BlockSpec((tm, k), lambda j, i: (i, 0)),
                  pl.BlockSpec((None, k, tn), lambda j, i: (layer, 0, j)),
                  tab_spec, tab_spec],
        out_specs=pl.BlockSpec((tm, tn), lambda j, i: (i, j)),
        out_shape=jax.ShapeDtypeStruct((t, n), BF16),
        scratch_shapes=[pltpu.VMEM((k, tn), BF16)],
        compiler_params=_params("arbitrary", "arbitrary"),
        name="qkv_rope",
    )(a, w, cos_tab, sin_tab)


def _swiglu_kernel(a_ref, wg_ref, wu_ref, o_ref, wg_bf, wu_bf):
    @pl.when(pl.program_id(1) == 0)
    def _():
        wg_bf[...] = wg_ref[...].astype(BF16)
        wu_bf[...] = wu_ref[...].astype(BF16)

    for r in range(a_ref.shape[0] // MM_SUB_ROWS):
        rows = slice(r * MM_SUB_ROWS, (r + 1) * MM_SUB_ROWS)
        a = a_ref[rows, :]
        for c in range(wg_bf.shape[1] // MM_SUB_COLS):
            cols = slice(c * MM_SUB_COLS, (c + 1) * MM_SUB_COLS)
            g = jnp.dot(a, wg_bf[:, cols], preferred_element_type=F32)
            u = jnp.dot(a, wu_bf[:, cols], preferred_element_type=F32)
            o_ref[rows, cols] = (g * jax.nn.sigmoid(g) * u).astype(BF16)


def _swiglu_in(a, w_in, layer, tm=2048, tn=512):
    t, k = a.shape
    hidden = w_in.shape[2] // 2
    nj = hidden // tn
    return pl.pallas_call(
        _swiglu_kernel,
        grid=(nj, t // tm),
        in_specs=[pl.BlockSpec((tm, k), lambda j, i: (i, 0)),
                  pl.BlockSpec((None, k, tn), lambda j, i: (layer, 0, j)),
                  pl.BlockSpec((None, k, tn), lambda j, i: (layer, 0, j + nj))],
        out_specs=pl.BlockSpec((tm, tn), lambda j, i: (i, j)),
        out_shape=jax.ShapeDtypeStruct((t, hidden), BF16),
        scratch_shapes=[pltpu.VMEM((k, tn), BF16), pltpu.VMEM((k, tn), BF16)],
        compiler_params=_params("arbitrary", "arbitrary"),
        name="swiglu_in",
    )(a, w_in, w_in)


def _res_kernel(out_x, next_norm, a_ref, w_ref, x_ref, gate_ref, *refs):
    refs = list(refs)
    if next_norm == "mod":
        g_ref, shift_ref, scale_ref = refs[:3]
        outs = refs[3:]
    else:
        g_ref = refs[0]
        outs = refs[1:]
    ox_ref = outs.pop(0) if out_x else None
    oh_ref = outs[0]
    if next_norm == "mod":
        gain = (g_ref[...] * (1 + scale_ref[...])).astype(BF16)
        shift = shift_ref[...].astype(BF16)
    for r in range(a_ref.shape[0] // RES_SUB_ROWS):
        rows = slice(r * RES_SUB_ROWS, (r + 1) * RES_SUB_ROWS)
        y = jnp.dot(a_ref[rows, :], w_ref[...], preferred_element_type=F32)
        xn = x_ref[rows, :] + gate_ref[...] * y
        if out_x:
            ox_ref[rows, :] = xn
        if next_norm == "mod":
            ms = jnp.mean(xn * xn, axis=-1, keepdims=True)
            oh_ref[rows, :] = (xn * lax.rsqrt(ms + NORM_EPS)).astype(BF16) * gain + shift
        else:
            oh_ref[rows, :] = _rms(xn, g_ref[...])


def _res(a, w, layer, x, gate, g, shift=None, scale=None, *, out_x=True, tm):
    t, k = a.shape
    d = w.shape[2]
    next_norm = "final" if shift is None else "mod"
    row = pl.BlockSpec((tm, d), lambda i: (i, 0))
    in_specs = [pl.BlockSpec((tm, k), lambda i: (i, 0)),
                pl.BlockSpec((None, k, d), lambda i: (layer, 0, 0), pipeline_mode=pl.Buffered(1)),
                row, _group_spec(gate, t, tm), pl.BlockSpec((1, d), lambda i: (0, 0))]
    args = [a, w, x, gate.table, g.reshape(1, d)]
    if next_norm == "mod":
        in_specs += [_group_spec(shift, t, tm), _group_spec(scale, t, tm)]
        args += [shift.table, scale.table]
    out_specs, out_shape = [], []
    if out_x:
        out_specs.append(row)
        out_shape.append(jax.ShapeDtypeStruct((t, d), F32))
    out_specs.append(row)
    out_shape.append(jax.ShapeDtypeStruct((t, d), BF16 if next_norm == "mod" else F32))
    return pl.pallas_call(
        functools.partial(_res_kernel, out_x, next_norm),
        grid=(t // tm,),
        in_specs=in_specs, out_specs=out_specs, out_shape=out_shape,
        compiler_params=_params("arbitrary"),
        name="proj_residual",
    )(*args)


def _nt_dot(q, k):
    return lax.dot_general(q, k, (((1,), (1,)), ((), ())), preferred_element_type=F32)


def _softmax_pv(q, k_parts, v_refs):
    scores = [_nt_dot(q, k) for k in k_parts]
    mx = functools.reduce(jnp.maximum, [jnp.max(s, axis=1, keepdims=True) for s in scores])
    e = [jnp.exp2(s - mx) for s in scores]
    tot = functools.reduce(jnp.add, [jnp.sum(x, axis=1, keepdims=True) for x in e])
    o = functools.reduce(jnp.add, [jnp.dot(p.astype(BF16), v[...], preferred_element_type=F32)
                                   for p, v in zip(e, v_refs)])
    return o, 1.0 / tot


def _da_attn_kernel(nseg, lam_init, lamv_ref, g_ref, q_ref, *refs):
    k_refs, v_refs, o_ref = refs[:nseg], refs[nseg:2 * nseg], refs[2 * nseg]
    lv = lamv_ref[...]
    lam = (jnp.exp(jnp.sum(lv[0:1] * lv[1:2], axis=1, keepdims=True))
           - jnp.exp(jnp.sum(lv[2:3] * lv[3:4], axis=1, keepdims=True)) + lam_init)
    for hh in range(q_ref.shape[1] // DA_V_DIM):
        vcols = slice(hh * DA_V_DIM, (hh + 1) * DA_V_DIM)
        vs = [v.at[:, vcols] for v in v_refs]
        for t in range(q_ref.shape[0] // ATTN_SUB_ROWS):
            rows = slice(t * ATTN_SUB_ROWS, (t + 1) * ATTN_SUB_ROWS)
            o = None
            for m in range(2):
                sl = slice(hh * DA_V_DIM + m * DA_HEAD_DIM, hh * DA_V_DIM + (m + 1) * DA_HEAD_DIM)
                om, r = _softmax_pv(q_ref[rows, sl], [k[:, sl] for k in k_refs], vs)
                o = om * r if m == 0 else o - om * (lam * r)
            o_ref[rows, vcols] = (_rms(o, g_ref[...]) * (1.0 - lam_init)).astype(BF16)


def _da_attention(q_src, kv_srcs, lam_vecs, subln_g, lam_init, heads, tq, heads_per_step=1):
    b, sq, _ = q_src.shape
    hd = heads_per_step * DA_V_DIM
    groups = heads // heads_per_step
    in_specs = [pl.BlockSpec((4, DA_HEAD_DIM), lambda bi, h, qi: (0, 0)),
                pl.BlockSpec((1, DA_V_DIM), lambda bi, h, qi: (0, 0)),
                pl.BlockSpec((None, tq, hd), lambda bi, h, qi: (bi, qi, h))]
    args = [lam_vecs, subln_g.reshape(1, DA_V_DIM), q_src]
    for kind in (1, 2):
        for src in kv_srcs:
            in_specs.append(pl.BlockSpec((None, src.shape[1], hd),
                                         lambda bi, h, qi, kind=kind: (bi, 0, kind * groups + h)))
            args.append(src)
    return pl.pallas_call(
        functools.partial(_da_attn_kernel, len(kv_srcs), lam_init),
        grid=(b, groups, sq // tq),
        in_specs=in_specs,
        out_specs=pl.BlockSpec((None, tq, hd), lambda bi, h, qi: (bi, qi, h)),
        out_shape=jax.ShapeDtypeStruct((b, sq, heads * DA_V_DIM), BF16),
        compiler_params=_params("arbitrary", "arbitrary", "arbitrary"),
        name="diff_attention",
    )(*args)


def _mla_attn_kernel(nseg, q_ref, *refs):
    k_refs, v_refs, o_ref = refs[:nseg], refs[nseg:2 * nseg], refs[2 * nseg]
    for t in range(q_ref.shape[0] // ATTN_SUB_ROWS):
        rows = slice(t * ATTN_SUB_ROWS, (t + 1) * ATTN_SUB_ROWS)
        for hh in range(2):
            cols = slice(hh * MLA_QK_PAD, (hh + 1) * MLA_QK_PAD)
            o, r = _softmax_pv(q_ref[rows, cols], [k[:, cols] for k in k_refs], v_refs)
            half = slice(hh * MLA_V, (hh + 1) * MLA_V)
            o_ref[rows, half] = (o[:, half] * r).astype(BF16)


def _mla_attention(q, ks, vs, heads, tq):
    b, sq, _ = q.shape
    in_specs = [pl.BlockSpec((None, tq, 2 * MLA_QK_PAD), lambda bi, h, qi: (bi, qi, h))]
    in_specs += [pl.BlockSpec((None, k.shape[1], 2 * MLA_QK_PAD), lambda bi, h, qi: (bi, 0, h)) for k in ks]
    in_specs += [pl.BlockSpec((None, v.shape[1], 2 * MLA_V), lambda bi, h, qi: (bi, 0, h)) for v in vs]
    return pl.pallas_call(
        functools.partial(_mla_attn_kernel, len(ks)),
        grid=(b, heads // 2, sq // tq),
        in_specs=in_specs,
        out_specs=pl.BlockSpec((None, tq, 2 * MLA_V), lambda bi, h, qi: (bi, qi, h)),
        out_shape=jax.ShapeDtypeStruct((b, sq, heads * MLA_V), BF16),
        compiler_params=_params("arbitrary", "arbitrary", "arbitrary"),
        name="mla_attention",
    )(q, *ks, *vs)


def _mla_proj_kernel(with_q, a_ref, win_ref, qg_ref, kvg_ref, cos_ref, sin_ref, wk_ref, wv_ref, *refs):
    if with_q:
        wq_ref, q_ref, k_ref, v_ref = refs
    else:
        k_ref, v_ref = refs
    cos = cos_ref[...]
    sin = sin_ref[...]
    acc = jnp.dot(a_ref[...], win_ref[...], preferred_element_type=F32)
    ckv = _rms(acc[:, MLA_Q_LORA:MLA_Q_LORA + MLA_KV_LORA], kvg_ref[...]).astype(BF16)
    kr = acc[:, MLA_Q_LORA + MLA_KV_LORA:]
    kr = (kr * cos + _rot_half(kr) * sin).astype(BF16)
    v_ref[...] = jnp.dot(ckv, wv_ref[...], preferred_element_type=F32).astype(BF16)
    kn = jnp.dot(ckv, wk_ref[...], preferred_element_type=F32).astype(BF16)
    for h in range(kn.shape[1] // MLA_NOPE):
        base = h * MLA_QK_PAD
        k_ref[:, base:base + MLA_NOPE] = kn[:, h * MLA_NOPE:(h + 1) * MLA_NOPE]
        k_ref[:, base + MLA_NOPE:base + MLA_QK_PAD] = kr
    if with_q:
        cq = _rms(acc[:, :MLA_Q_LORA], qg_ref[...]).astype(BF16)
        qcos = cos * MLA_Q_SCALE
        qsin = sin * MLA_Q_SCALE
        for c in range(wq_ref.shape[1] // Q_UP_SUB_COLS):
            qa = jnp.dot(cq, wq_ref[:, c * Q_UP_SUB_COLS:(c + 1) * Q_UP_SUB_COLS], preferred_element_type=F32)
            for h in range(Q_UP_SUB_COLS // MLA_QK_PAD):
                lo = h * MLA_QK_PAD
                base = c * Q_UP_SUB_COLS + lo
                q_ref[:, base:base + MLA_NOPE] = (qa[:, lo:lo + MLA_NOPE] * MLA_Q_SCALE).astype(BF16)
                xr = qa[:, lo + MLA_NOPE:lo + MLA_QK_PAD]
                q_ref[:, base + MLA_NOPE:base + MLA_QK_PAD] = (xr * qcos + _rot_half(xr) * qsin).astype(BF16)


def _mla_proj(a, w_in, q_g, kv_g, cos_tab, sin_tab, wk, wv, wq=None, tm=512):
    t, k = a.shape
    with_q = wq is not None
    row_tiles = cos_tab.shape[0] // tm
    heads = wk.shape[1] // MLA_NOPE
    const = lambda arr: pl.BlockSpec(arr.shape, lambda i: (0, 0), pipeline_mode=pl.Buffered(1))
    tab = pl.BlockSpec((tm, LANES), lambda i: (i % row_tiles, 0))
    in_specs = [pl.BlockSpec((tm, k), lambda i: (i, 0)), const(w_in),
                pl.BlockSpec((1, MLA_Q_LORA), lambda i: (0, 0)),
                pl.BlockSpec((1, MLA_KV_LORA), lambda i: (0, 0)),
                tab, tab, const(wk), const(wv)]
    args = [a, w_in, q_g.reshape(1, -1), kv_g.reshape(1, -1), cos_tab, sin_tab, wk, wv]
    widths = [heads * MLA_QK_PAD, wv.shape[1]]
    if with_q:
        in_specs.append(const(wq))
        args.append(wq)
        widths.insert(0, wq.shape[1])
    return pl.pallas_call(
        functools.partial(_mla_proj_kernel, with_q),
        grid=(t // tm,),
        in_specs=in_specs,
        out_specs=[pl.BlockSpec((tm, w_), lambda i: (i, 0)) for w_ in widths],
        out_shape=[jax.ShapeDtypeStruct((t, w_), BF16) for w_ in widths],
        compiler_params=_params("arbitrary"),
        name="mla_proj",
    )(*args)


def _axial_angles(rows, rot_dim):
    row = jnp.repeat(jnp.arange(rows, dtype=F32), GRID_W)
    col = jnp.tile(jnp.arange(GRID_W, dtype=F32), rows)
    n_freq = rot_dim // 4
    inv_freq = ROPE_BASE ** (-jnp.arange(n_freq, dtype=F32) / n_freq)
    ang = jnp.concatenate([row[:, None] * inv_freq, col[:, None] * inv_freq], axis=-1)
    return jnp.cos(ang), jnp.sin(ang)


def _da_tables(seq, ctx_rows):
    cos, sin = _axial_angles(seq // GRID_W, DA_HEAD_DIM)
    cos = jnp.concatenate([cos, cos], axis=-1)
    sin = jnp.concatenate([-sin, sin], axis=-1)
    one, zero = jnp.ones_like(cos), jnp.zeros_like(cos)
    lat = (jnp.stack([cos * DA_Q_SCALE, cos, one]), jnp.stack([sin * DA_Q_SCALE, sin, zero]))
    one_c, zero_c = one[:ctx_rows], zero[:ctx_rows]
    ctx = (jnp.stack([one_c * DA_Q_SCALE, one_c, one_c]), jnp.stack([zero_c, zero_c, zero_c]))
    return lat, ctx


def _spread(x1, x2):
    z = jnp.zeros_like(x1)
    return jnp.concatenate([x1, z, x2, z], axis=-1)


def _mla_tables(seq, ctx_rows):
    cos, sin = _axial_angles(seq // GRID_W, MLA_ROPE)
    cos_l, sin_l = _spread(cos, cos), _spread(-sin, sin)
    cos_c, sin_c = jnp.ones_like(cos_l[:ctx_rows]), jnp.zeros_like(cos_l[:ctx_rows])
    return (cos_l, sin_l), (cos_c, sin_c)


def _mla_weight_layouts(w_in, w_q_up, w_kv_up):
    half = MLA_ROPE // 2
    lora = MLA_Q_LORA + MLA_KV_LORA
    w_in_p = jnp.concatenate([w_in[:, :lora], _spread(w_in[:, lora:lora + half], w_in[:, lora + half:])], axis=1)
    wq = w_q_up.reshape(MLA_Q_LORA, -1, MLA_NOPE + MLA_ROPE)
    wq_p = jnp.concatenate([wq[..., :MLA_NOPE],
                            _spread(wq[..., MLA_NOPE:MLA_NOPE + half], wq[..., MLA_NOPE + half:])], axis=-1)
    wkv = w_kv_up.reshape(MLA_KV_LORA, -1, MLA_NOPE + MLA_V)
    wk = wkv[..., :MLA_NOPE].reshape(MLA_KV_LORA, -1)
    wv = wkv[..., MLA_NOPE:].reshape(MLA_KV_LORA, -1)
    return w_in_p, wq_p.reshape(MLA_Q_LORA, -1), wk, wv


def kernel(x, c, ctx, c_ctx, ada_w, ada_b, norm_mix_g, norm_ffn_g, ffn_w_in, ffn_w_out, da_w_qkv, da_lambda, da_subln_g, da_w_o, mla_w_in, mla_q_norm_g, mla_w_q_up, mla_kv_norm_g, mla_w_kv_up, mla_w_o, final_norm_g):
    b, s, d = x.shape
    cl = ctx.shape[1]
    assert ada_w.shape[0] == 2 and b + 1 <= MOD_ROWS
    da_heads = d // DA_V_DIM
    mla_heads = d // MLA_V

    cond = jnp.concatenate([c, c_ctx[None], jnp.zeros((MOD_ROWS - b - 1, d), F32)], axis=0)
    mods = _mods(cond, ada_w, ada_b).reshape(ada_w.shape[0], MOD_ROWS, 6, 1, d)

    def lat_mod(i, k):
        return _Mod(mods, i, k, 0, b)

    def ctx_mod(i, k):
        return _Mod(mods, i, k, b, 1)

    bf = lambda w: w.astype(BF16)
    w_o0, w_o1, w_ffn_out = bf(da_w_o), bf(mla_w_o), bf(ffn_w_out)
    w_in_p, wq_p, wk, wv = _mla_weight_layouts(bf(mla_w_in[0]), bf(mla_w_q_up[0]), bf(mla_w_kv_up[0]))
    da_lat_tab, da_ctx_tab = _da_tables(s, 1024)
    mla_lat_tab, mla_ctx_tab = _mla_tables(s, 512)

    xl = x.reshape(b * s, d)
    xc = ctx.reshape(b * cl, d)

    lam_init = 0.8 - 0.6 * math.exp(-0.3 * 0)
    h_l = _norm_mod(xl, norm_mix_g[0], lat_mod(0, 0), lat_mod(0, 1))
    h_c = _norm_mod(xc, norm_mix_g[0], ctx_mod(0, 0), ctx_mod(0, 1))
    qkv_l = _qkv_rope(h_l, da_w_qkv, 0, *da_lat_tab).reshape(b, s, -1)
    qkv_c = _qkv_rope(h_c, da_w_qkv, 0, *da_ctx_tab).reshape(b, cl, -1)
    a_l = _da_attention(qkv_l, [qkv_c, qkv_l], da_lambda[0], da_subln_g[0], lam_init, da_heads, tq=s)
    a_c = _da_attention(qkv_c, [qkv_c], da_lambda[0], da_subln_g[0], lam_init, da_heads, tq=cl,
                        heads_per_step=da_heads)
    xl, hf_l = _res(a_l.reshape(b * s, d), w_o0, 0, xl, lat_mod(0, 2), norm_ffn_g[0], lat_mod(0, 3), lat_mod(0, 4), tm=512)
    xc, hf_c = _res(a_c.reshape(b * cl, d), w_o0, 0, xc, ctx_mod(0, 2), norm_ffn_g[0], ctx_mod(0, 3), ctx_mod(0, 4), tm=512)
    act_l = _swiglu_in(hf_l, ffn_w_in, 0)
    act_c = _swiglu_in(hf_c, ffn_w_in, 0)
    xl, hm_l = _res(act_l, w_ffn_out, 0, xl, lat_mod(0, 5), norm_mix_g[1], lat_mod(1, 0), lat_mod(1, 1), tm=256)
    (hm_c,) = _res(act_c, w_ffn_out, 0, xc, ctx_mod(0, 5), norm_mix_g[1], ctx_mod(1, 0), ctx_mod(1, 1),
                   out_x=False, tm=512)

    q_l, k_l, v_l = _mla_proj(hm_l, w_in_p, mla_q_norm_g[0], mla_kv_norm_g[0], *mla_lat_tab, wk, wv, wq_p)
    k_c, v_c = _mla_proj(hm_c, w_in_p, mla_q_norm_g[0], mla_kv_norm_g[0], *mla_ctx_tab, wk, wv)
    a_l = _mla_attention(q_l.reshape(b, s, -1),
                         [k_c.reshape(b, cl, -1), k_l.reshape(b, s, -1)],
                         [v_c.reshape(b, cl, -1), v_l.reshape(b, s, -1)], mla_heads, tq=s)
    xl, hf_l = _res(a_l.reshape(b * s, d), w_o1, 0, xl, lat_mod(1, 2), norm_ffn_g[1], lat_mod(1, 3), lat_mod(1, 4), tm=512)
    act_l = _swiglu_in(hf_l, ffn_w_in, 1)
    (out,) = _res(act_l, w_ffn_out, 1, xl, lat_mod(1, 5), final_norm_g, out_x=False, tm=512)
    return out.reshape(b, s, d)
```

```python
import functools
import math
from typing import NamedTuple

import jax
import jax.numpy as jnp
from jax import lax
from jax.experimental import pallas as pl
from jax.experimental.pallas import tpu as pltpu

F32 = jnp.float32
BF16 = jnp.bfloat16

GRID_W = 64
NORM_EPS = 1e-6
ROPE_BASE = 10000.0
DA_HEAD_DIM = 128
DA_V_DIM = 2 * DA_HEAD_DIM
DA_SCALE = DA_HEAD_DIM ** -0.5
MLA_NOPE = 128
MLA_ROPE = 64
MLA_V = 128
MLA_Q_LORA = 512
MLA_KV_LORA = 512
MLA_SCALE = (MLA_NOPE + MLA_ROPE) ** -0.5
MLA_QK_PAD = 256
LOG2_E = math.log2(math.e)
DA_Q_SCALE = DA_SCALE * LOG2_E
MLA_Q_SCALE = MLA_SCALE * LOG2_E
ATTN_SUB_ROWS = 256
MM_SUB_ROWS = 256
MM_SUB_COLS = 256
RES_SUB_ROWS = 256
Q_UP_SUB_COLS = 1024

LANES = 128
V7X_VMEM_BYTES = 64 * 1024 * 1024
VMEM_LIMIT = V7X_VMEM_BYTES - 8 * 1024 * 1024

MOD_ROWS = 16


def _params(*sem):
    return pltpu.CompilerParams(dimension_semantics=sem, vmem_limit_bytes=VMEM_LIMIT)


def _rms(x, g):
    ms = jnp.mean(x * x, axis=-1, keepdims=True)
    return x * lax.rsqrt(ms + NORM_EPS) * g


def _rot_half(x):
    return pltpu.roll(x, LANES // 2, 1)


def _mod_kernel(c_ref, w_ref, b_ref, o_ref):
    c = c_ref[...]
    s = c * jax.nn.sigmoid(c)
    o_ref[0] = jnp.dot(s.astype(BF16), w_ref[0].astype(BF16), preferred_element_type=F32) + b_ref[0]


def _mods(cond, ada_w, ada_b, tn=1024):
    depth, d, n = ada_w.shape
    return pl.pallas_call(
        _mod_kernel,
        grid=(depth, n // tn),
        in_specs=[pl.BlockSpec((MOD_ROWS, d), lambda l, j: (0, 0)),
                  pl.BlockSpec((1, d, tn), lambda l, j: (l, 0, j)),
                  pl.BlockSpec((1, 1, tn), lambda l, j: (l, 0, j))],
        out_specs=pl.BlockSpec((1, MOD_ROWS, tn), lambda l, j: (l, 0, j)),
        out_shape=jax.ShapeDtypeStruct((depth, MOD_ROWS, n), F32),
        compiler_params=_params("arbitrary", "arbitrary"),
        name="adaln_mod",
    )(cond, ada_w, ada_b.reshape(depth, 1, n))


def _norm_mod_kernel(x_ref, g_ref, shift_ref, scale_ref, o_ref):
    y = _rms(x_ref[...], g_ref[...])
    o_ref[...] = (y * (1 + scale_ref[...]) + shift_ref[...]).astype(BF16)


class _Mod(NamedTuple):
    table: jax.Array
    layer: int
    k: int
    row0: int
    groups: int


def _group_spec(m, t, tm):
    tiles_per_group = t // m.groups // tm
    d = m.table.shape[-1]
    return pl.BlockSpec((None, None, None, 1, d),
                        lambda i, *_: (m.layer, m.row0 + i // tiles_per_group, m.k, 0, 0))


def _norm_mod(x, g, shift, scale, tm=1024):
    t, d = x.shape
    return pl.pallas_call(
        _norm_mod_kernel,
        grid=(t // tm,),
        in_specs=[pl.BlockSpec((tm, d), lambda i: (i, 0)),
                  pl.BlockSpec((1, d), lambda i: (0, 0)),
                  _group_spec(shift, t, tm), _group_spec(scale, t, tm)],
        out_specs=pl.BlockSpec((tm, d), lambda i: (i, 0)),
        out_shape=jax.ShapeDtypeStruct((t, d), BF16),
        compiler_params=_params("arbitrary"),
        name="norm_mod",
    )(x, g.reshape(1, d), shift.table, scale.table)


def _qkv_rope_kernel(a_ref, w_ref, cos_ref, sin_ref, o_ref, w_bf):
    @pl.when(pl.program_id(1) == 0)
    def _():
        w_bf[...] = w_ref[...].astype(BF16)

    for r in range(a_ref.shape[0] // MM_SUB_ROWS):
        rows = slice(r * MM_SUB_ROWS, (r + 1) * MM_SUB_ROWS)
        acc = jnp.dot(a_ref[rows, :], w_bf[...], preferred_element_type=F32)
        cos = cos_ref[rows, :]
        sin = sin_ref[rows, :]
        for g in range(acc.shape[1] // LANES):
            sl = slice(g * LANES, (g + 1) * LANES)
            xg = acc[:, sl]
            o_ref[rows, sl] = (xg * cos + _rot_half(xg) * sin).astype(BF16)


def _qkv_rope(a, w, layer, cos_tab, sin_tab, tm=1024, tn=1024):
    t, k = a.shape
    n = w.shape[2]
    tiles_per_kind = n // 3 // tn
    row_tiles = cos_tab.shape[1] // tm
    tab_spec = pl.BlockSpec((None, tm, LANES), lambda j, i: (j // tiles_per_kind, i % row_tiles, 0))
    return pl.pallas_call(
        _qkv_rope_kernel,
        grid=(n // tn, t // tm),
        in_specs=[pl.BlockSpec((tm, k), lambda j, i: (i, 0)),
                  pl.BlockSpec((None, k, tn), lambda j, i: (layer, 0, j)),
                  tab_spec, tab_spec],
        out_specs=pl.BlockSpec((tm, tn), lambda j, i: (i, j)),
        out_shape=jax.ShapeDtypeStruct((t, n), BF16),
        scratch_shapes=[pltpu.VMEM((k, tn), BF16)],
        compiler_params=_params("arbitrary", "arbitrary"),
        name="qkv_rope",
    )(a, w, cos_tab, sin_tab)


def _swiglu_kernel(a_ref, wg_ref, wu_ref, o_ref, wg_bf, wu_bf):
    @pl.when(pl.program_id(1) == 0)
    def _():
        wg_bf[...] = wg_ref[...].astype(BF16)
        wu_bf[...] = wu_ref[...].astype(BF16)

    for r in range(a_ref.shape[0] // MM_SUB_ROWS):
        rows = slice(r * MM_SUB_ROWS, (r + 1) * MM_SUB_ROWS)
        a = a_ref[rows, :]
        for c in range(wg_bf.shape[1] // MM_SUB_COLS):
            cols = slice(c * MM_SUB_COLS, (c + 1) * MM_SUB_COLS)
            g = jnp.dot(a, wg_bf[:, cols], preferred_element_type=F32)
            u = jnp.dot(a, wu_bf[:, cols], preferred_element_type=F32)
            o_ref[rows, cols] = (g * jax.nn.sigmoid(g) * u).astype(BF16)


def _swiglu_in(a, w_in, layer, tm=2048, tn=512):
    t, k = a.shape
    hidden = w_in.shape[2] // 2
    nj = hidden // tn
    return pl.pallas_call(
        _swiglu_kernel,
        grid=(nj, t // tm),
        in_specs=[pl.BlockSpec((tm, k), lambda j, i: (i, 0)),
                  pl.BlockSpec((None, k, tn), lambda j, i: (layer, 0, j)),
                  pl.BlockSpec((None, k, tn), lambda j, i: (layer, 0, j + nj))],
        out_specs=pl.BlockSpec((tm, tn), lambda j, i: (i, j)),
        out_shape=jax.ShapeDtypeStruct((t, hidden), BF16),
        scratch_shapes=[pltpu.VMEM((k, tn), BF16), pltpu.VMEM((k, tn), BF16)],
        compiler_params=_params("arbitrary", "arbitrary"),
        name="swiglu_in",
    )(a, w_in, w_in)


def _res_kernel(out_x, next_norm, a_ref, w_ref, x_ref, gate_ref, *refs):
    refs = list(refs)
    if next_norm == "mod":
        g_ref, shift_ref, scale_ref = refs[:3]
        outs = refs[3:]
    else:
        g_ref = refs[0]
        outs = refs[1:]
    ox_ref = outs.pop(0) if out_x else None
    oh_ref = outs[0]
    if next_norm == "mod":
        gain = (g_ref[...] * (1 + scale_ref[...])).astype(BF16)
        shift = shift_ref[...].astype(BF16)
    for r in range(a_ref.shape[0] // RES_SUB_ROWS):
        rows = slice(r * RES_SUB_ROWS, (r + 1) * RES_SUB_ROWS)
        y = jnp.dot(a_ref[rows, :], w_ref[...], preferred_element_type=F32)
        xn = x_ref[rows, :] + gate_ref[...] * y
        if out_x:
            ox_ref[rows, :] = xn
        if next_norm == "mod":
            ms = jnp.mean(xn * xn, axis=-1, keepdims=True)
            oh_ref[rows, :] = (xn * lax.rsqrt(ms + NORM_EPS)).astype(BF16) * gain + shift
        else:
            oh_ref[rows, :] = _rms(xn, g_ref[...])


def _res(a, w, layer, x, gate, g, shift=None, scale=None, *, out_x=True, tm):
    t, k = a.shape
    d = w.shape[2]
    next_norm = "final" if shift is None else "mod"
    row = pl.BlockSpec((tm, d), lambda i: (i, 0))
    in_specs = [pl.BlockSpec((tm, k), lambda i: (i, 0)),
                pl.BlockSpec((None, k, d), lambda i: (layer, 0, 0), pipeline_mode=pl.Buffered(1)),
                row, _group_spec(gate, t, tm), pl.BlockSpec((1, d), lambda i: (0, 0))]
    args = [a, w, x, gate.table, g.reshape(1, d)]
    if next_norm == "mod":
        in_specs += [_group_spec(shift, t, tm), _group_spec(scale, t, tm)]
        args += [shift.table, scale.table]
    out_specs, out_shape = [], []
    if out_x:
        out_specs.append(row)
        out_shape.append(jax.ShapeDtypeStruct((t, d), F32))
    out_specs.append(row)
    out_shape.append(jax.ShapeDtypeStruct((t, d), BF16 if next_norm == "mod" else F32))
    return pl.pallas_call(
        functools.partial(_res_kernel, out_x, next_norm),
        grid=(t // tm,),
        in_specs=in_specs, out_specs=out_specs, out_shape=out_shape,
        compiler_params=_params("arbitrary"),
        name="proj_residual",
    )(*args)


def _nt_dot(q, k):
    return lax.dot_general(q, k, (((1,), (1,)), ((), ())), preferred_element_type=F32)


def _softmax_pv(q, k_parts, v_refs):
    scores = [_nt_dot(q, k) for k in k_parts]
    mx = functools.reduce(jnp.maximum, [jnp.max(s, axis=1, keepdims=True) for s in scores])
    e = [jnp.exp2(s - mx) for s in scores]
    tot = functools.reduce(jnp.add, [jnp.sum(x, axis=1, keepdims=True) for x in e])
    o = functools.reduce(jnp.add, [jnp.dot(p.astype(BF16), v[...], preferred_element_type=F32)
                                   for p, v in zip(e, v_refs)])
    return o, 1.0 / tot


def _da_attn_kernel(nseg, lam_init, lamv_ref, g_ref, q_ref, *refs):
    k_refs, v_refs, o_ref = refs[:nseg], refs[nseg:2 * nseg], refs[2 * nseg]
    lv = lamv_ref[...]
    lam = (jnp.exp(jnp.sum(lv[0:1] * lv[1:2], axis=1, keepdims=True))
           - jnp.exp(jnp.sum(lv[2:3] * lv[3:4], axis=1, keepdims=True)) + lam_init)
    for hh in range(q_ref.shape[1] // DA_V_DIM):
        vcols = slice(hh * DA_V_DIM, (hh + 1) * DA_V_DIM)
        vs = [v.at[:, vcols] for v in v_refs]
        for t in range(q_ref.shape[0] // ATTN_SUB_ROWS):
            rows = slice(t * ATTN_SUB_ROWS, (t + 1) * ATTN_SUB_ROWS)
            o = None
            for m in range(2):
                sl = slice(hh * DA_V_DIM + m * DA_HEAD_DIM, hh * DA_V_DIM + (m + 1) * DA_HEAD_DIM)
                om, r = _softmax_pv(q_ref[rows, sl], [k[:, sl] for k in k_refs], vs)
                o = om * r if m == 0 else o - om * (lam * r)
            o_ref[rows, vcols] = (_rms(o, g_ref[...]) * (1.0 - lam_init)).astype(BF16)


def _da_attention(q_src, kv_srcs, lam_vecs, subln_g, lam_init, heads, tq, heads_per_step=1):
    b, sq, _ = q_src.shape
    hd = heads_per_step * DA_V_DIM
    groups = heads // heads_per_step
    in_specs = [pl.BlockSpec((4, DA_HEAD_DIM), lambda bi, h, qi: (0, 0)),
                pl.BlockSpec((1, DA_V_DIM), lambda bi, h, qi: (0, 0)),
                pl.BlockSpec((None, tq, hd), lambda bi, h, qi: (bi, qi, h))]
    args = [lam_vecs, subln_g.reshape(1, DA_V_DIM), q_src]
    for kind in (1, 2):
        for src in kv_srcs:
            in_specs.append(pl.BlockSpec((None, src.shape[1], hd),
                                         lambda bi, h, qi, kind=kind: (bi, 0, kind * groups + h)))
            args.append(src)
    return pl.pallas_call(
        functools.partial(_da_attn_kernel, len(kv_srcs), lam_init),
        grid=(b, groups, sq // tq),
        in_specs=in_specs,
        out_specs=pl.BlockSpec((None, tq, hd), lambda bi, h, qi: (bi, qi, h)),
        out_shape=jax.ShapeDtypeStruct((b, sq, heads * DA_V_DIM), BF16),
        compiler_params=_params("arbitrary", "arbitrary", "arbitrary"),
        name="diff_attention",
    )(*args)


def _mla_attn_kernel(nseg, q_ref, *refs):
    k_refs, v_refs, o_ref = refs[:nseg], refs[nseg:2 * nseg], refs[2 * nseg]
    for t in range(q_ref.shape[0] // ATTN_SUB_ROWS):
        rows = slice(t * ATTN_SUB_ROWS, (t + 1) * ATTN_SUB_ROWS)
        for hh in range(2):
            cols = slice(hh * MLA_QK_PAD, (hh + 1) * MLA_QK_PAD)
            o, r = _softmax_pv(q_ref[rows, cols], [k[:, cols] for k in k_refs], v_refs)
            half = slice(hh * MLA_V, (hh + 1) * MLA_V)
            o_ref[rows, half] = (o[:, half] * r).astype(BF16)


def _mla_attention(q, ks, vs, heads, tq):
    b, sq, _ = q.shape
    in_specs = [pl.BlockSpec((None, tq, 2 * MLA_QK_PAD), lambda bi, h, qi: (bi, qi, h))]
    in_specs += [pl.BlockSpec((None, k.shape[1], 2 * MLA_QK_PAD), lambda bi, h, qi: (bi, 0, h)) for k in ks]
    in_specs += [pl.BlockSpec((None, v.shape[1], 2 * MLA_V), lambda bi, h, qi: (bi, 0, h)) for v in vs]
    return pl.pallas_call(
        functools.partial(_mla_attn_kernel, len(ks)),
        grid=(b, heads // 2, sq // tq),
        in_specs=in_specs,
        out_specs=pl.BlockSpec((None, tq, 2 * MLA_V), lambda bi, h, qi: (bi, qi, h)),
        out_shape=jax.ShapeDtypeStruct((b, sq, heads * MLA_V), BF16),
        compiler_params=_params("arbitrary", "arbitrary", "arbitrary"),
        name="mla_attention",
    )(q, *ks, *vs)


def _mla_proj_kernel(with_q, a_ref, win_ref, qg_ref, kvg_ref, cos_ref, sin_ref, wk_ref, wv_ref, *refs):
    if with_q:
        wq_ref, q_ref, k_ref, v_ref = refs
    else:
        k_ref, v_ref = refs
    cos = cos_ref[...]
    sin = sin_ref[...]
    acc = jnp.dot(a_ref[...], win_ref[...], preferred_element_type=F32)
    ckv = _rms(acc[:, MLA_Q_LORA:MLA_Q_LORA + MLA_KV_LORA], kvg_ref[...]).astype(BF16)
    kr = acc[:, MLA_Q_LORA + MLA_KV_LORA:]
    kr = (kr * cos + _rot_half(kr) * sin).astype(BF16)
    v_ref[...] = jnp.dot(ckv, wv_ref[...], preferred_element_type=F32).astype(BF16)
    kn = jnp.dot(ckv, wk_ref[...], preferred_element_type=F32).astype(BF16)
    for h in range(kn.shape[1] // MLA_NOPE):
        base = h * MLA_QK_PAD
        k_ref[:, base:base + MLA_NOPE] = kn[:, h * MLA_NOPE:(h + 1) * MLA_NOPE]
        k_ref[:, base + MLA_NOPE:base + MLA_QK_PAD] = kr
    if with_q:
        cq = _rms(acc[:, :MLA_Q_LORA], qg_ref[...]).astype(BF16)
        qcos = cos * MLA_Q_SCALE
        qsin = sin * MLA_Q_SCALE
        for c in range(wq_ref.shape[1] // Q_UP_SUB_COLS):
            qa = jnp.dot(cq, wq_ref[:, c * Q_UP_SUB_COLS:(c + 1) * Q_UP_SUB_COLS], preferred_element_type=F32)
            for h in range(Q_UP_SUB_COLS // MLA_QK_PAD):
                lo = h * MLA_QK_PAD
                base = c * Q_UP_SUB_COLS + lo
                q_ref[:, base:base + MLA_NOPE] = (qa[:, lo:lo + MLA_NOPE] * MLA_Q_SCALE).astype(BF16)
                xr = qa[:, lo + MLA_NOPE:lo + MLA_QK_PAD]
                q_ref[:, base + MLA_NOPE:base + MLA_QK_PAD] = (xr * qcos + _rot_half(xr) * qsin).astype(BF16)


def _mla_proj(a, w_in, q_g, kv_g, cos_tab, sin_tab, wk, wv, wq=None, tm=512):
    t, k = a.shape
    with_q = wq is not None
    row_tiles = cos_tab.shape[0] // tm
    heads = wk.shape[1] // MLA_NOPE
    const = lambda arr: pl.BlockSpec(arr.shape, lambda i: (0, 0), pipeline_mode=pl.Buffered(1))
    tab = pl.BlockSpec((tm, LANES), lambda i: (i % row_tiles, 0))
    in_specs = [pl.BlockSpec((tm, k), lambda i: (i, 0)), const(w_in),
                pl.BlockSpec((1, MLA_Q_LORA), lambda i: (0, 0)),
                pl.BlockSpec((1, MLA_KV_LORA), lambda i: (0, 0)),
                tab, tab, const(wk), const(wv)]
    args = [a, w_in, q_g.reshape(1, -1), kv_g.reshape(1, -1), cos_tab, sin_tab, wk, wv]
    widths = [heads * MLA_QK_PAD, wv.shape[1]]
    if with_q:
        in_specs.append(const(wq))
        args.append(wq)
        widths.insert(0, wq.shape[1])
    return pl.pallas_call(
        functools.partial(_mla_proj_kernel, with_q),
        grid=(t // tm,),
        in_specs=in_specs,
        out_specs=[pl.BlockSpec((tm, w_), lambda i: (i, 0)) for w_ in widths],
        out_shape=[jax.ShapeDtypeStruct((t, w_), BF16) for w_ in widths],
        compiler_params=_params("arbitrary"),
        name="mla_proj",
    )(*args)


def _axial_angles(rows, rot_dim):
    row = jnp.repeat(jnp.arange(rows, dtype=F32), GRID_W)
    col = jnp.tile(jnp.arange(GRID_W, dtype=F32), rows)
    n_freq = rot_dim // 4
    inv_freq = ROPE_BASE ** (-jnp.arange(n_freq, dtype=F32) / n_freq)
    ang = jnp.concatenate([row[:, None] * inv_freq, col[:, None] * inv_freq], axis=-1)
    return jnp.cos(ang), jnp.sin(ang)


def _da_tables(seq, ctx_rows):
    cos, sin = _axial_angles(seq // GRID_W, DA_HEAD_DIM)
    cos = jnp.concatenate([cos, cos], axis=-1)
    sin = jnp.concatenate([-sin, sin], axis=-1)
    one, zero = jnp.ones_like(cos), jnp.zeros_like(cos)
    lat = (jnp.stack([cos * DA_Q_SCALE, cos, one]), jnp.stack([sin * DA_Q_SCALE, sin, zero]))
    one_c, zero_c = one[:ctx_rows], zero[:ctx_rows]
    ctx = (jnp.stack([one_c * DA_Q_SCALE, one_c, one_c]), jnp.stack([zero_c, zero_c, zero_c]))
    return lat, ctx


def _spread(x1, x2):
    z = jnp.zeros_like(x1)
    return jnp.concatenate([x1, z, x2, z], axis=-1)


def _mla_tables(seq, ctx_rows):
    cos, sin = _axial_angles(seq // GRID_W, MLA_ROPE)
    cos_l, sin_l = _spread(cos, cos), _spread(-sin, sin)
    cos_c, sin_c = jnp.ones_like(cos_l[:ctx_rows]), jnp.zeros_like(cos_l[:ctx_rows])
    return (cos_l, sin_l), (cos_c, sin_c)


def _mla_weight_layouts(w_in, w_q_up, w_kv_up):
    half = MLA_ROPE // 2
    lora = MLA_Q_LORA + MLA_KV_LORA
    w_in_p = jnp.concatenate([w_in[:, :lora], _spread(w_in[:, lora:lora + half], w_in[:, lora + half:])], axis=1)
    wq = w_q_up.reshape(MLA_Q_LORA, -1, MLA_NOPE + MLA_ROPE)
    wq_p = jnp.concatenate([wq[..., :MLA_NOPE],
                            _spread(wq[..., MLA_NOPE:MLA_NOPE + half], wq[..., MLA_NOPE + half:])], axis=-1)
    wkv = w_kv_up.reshape(MLA_KV_LORA, -1, MLA_NOPE + MLA_V)
    wk = wkv[..., :MLA_NOPE].reshape(MLA_KV_LORA, -1)
    wv = wkv[..., MLA_NOPE:].reshape(MLA_KV_LORA, -1)
    return w_in_p, wq_p.reshape(MLA_Q_LORA, -1), wk, wv


def kernel(x, c, ctx, c_ctx, ada_w, ada_b, norm_mix_g, norm_ffn_g, ffn_w_in, ffn_w_out, da_w_qkv, da_lambda, da_subln_g, da_w_o, mla_w_in, mla_q_norm_g, mla_w_q_up, mla_kv_norm_g, mla_w_kv_up, mla_w_o, final_norm_g):
    b, s, d = x.shape
    cl = ctx.shape[1]
    assert ada_w.shape[0] == 2 and b + 1 <= MOD_ROWS
    da_heads = d // DA_V_DIM
    mla_heads = d // MLA_V

    cond = jnp.concatenate([c, c_ctx[None], jnp.zeros((MOD_ROWS - b - 1, d), F32)], axis=0)
    mods = _mods(cond, ada_w, ada_b).reshape(ada_w.shape[0], MOD_ROWS, 6, 1, d)

    def lat_mod(i, k):
        return _Mod(mods, i, k, 0, b)

    def ctx_mod(i, k):
        return _Mod(mods, i, k, b, 1)

    bf = lambda w: w.astype(BF16)
    w_o0, w_o1, w_ffn_out = bf(da_w_o), bf(mla_w_o), bf(ffn_w_out)
    w_in_p, wq_p, wk, wv = _mla_weight_layouts(bf(mla_w_in[0]), bf(mla_w_q_up[0]), bf(mla_w_kv_up[0]))
    da_lat_tab, da_ctx_tab = _da_tables(s, 1024)
    mla_lat_tab, mla_ctx_tab = _mla_tables(s, 512)

    xl = x.reshape(b * s, d)
    xc = ctx.reshape(b * cl, d)

    lam_init = 0.8 - 0.6 * math.exp(-0.3 * 0)
    h_l = _norm_mod(xl, norm_mix_g[0], lat_mod(0, 0), lat_mod(0, 1))
    h_c = _norm_mod(xc, norm_mix_g[0], ctx_mod(0, 0), ctx_mod(0, 1))
    qkv_l = _qkv_rope(h_l, da_w_qkv, 0, *da_lat_tab).reshape(b, s, -1)
    qkv_c = _qkv_rope(h_c, da_w_qkv, 0, *da_ctx_tab).reshape(b, cl, -1)
    a_l = _da_attention(qkv_l, [qkv_c, qkv_l], da_lambda[0], da_subln_g[0], lam_init, da_heads, tq=s)
    a_c = _da_attention(qkv_c, [qkv_c], da_lambda[0], da_subln_g[0], lam_init, da_heads, tq=cl,
                        heads_per_step=da_heads)
    xl, hf_l = _res(a_l.reshape(b * s, d), w_o0, 0, xl, lat_mod(0, 2), norm_ffn_g[0], lat_mod(0, 3), lat_mod(0, 4), tm=512)
    xc, hf_c = _res(a_c.reshape(b * cl, d), w_o0, 0, xc, ctx_mod(0, 2), norm_ffn_g[0], ctx_mod(0, 3), ctx_mod(0, 4), tm=512)
    act_l = _swiglu_in(hf_l, ffn_w_in, 0)
    act_c = _swiglu_in(hf_c, ffn_w_in, 0)
    xl, hm_l = _res(act_l, w_ffn_out, 0, xl, lat_mod(0, 5), norm_mix_g[1], lat_mod(1, 0), lat_mod(1, 1), tm=512)
    (hm_c,) = _res(act_c, w_ffn_out, 0, xc, ctx_mod(0, 5), norm_mix_g[1], ctx_mod(1, 0), ctx_mod(1, 1),
                   out_x=False, tm=512)

    q_l, k_l, v_l = _mla_proj(hm_l, w_in_p, mla_q_norm_g[0], mla_kv_norm_g[0], *mla_lat_tab, wk, wv, wq_p)
    k_c, v_c = _mla_proj(hm_c, w_in_p, mla_q_norm_g[0], mla_kv_norm_g[0], *mla_ctx_tab, wk, wv)
    a_l = _mla_attention(q_l.reshape(b, s, -1),
                         [k_c.reshape(b, cl, -1), k_l.reshape(b, s, -1)],
                         [v_c.reshape(b, cl, -1), v_l.reshape(b, s, -1)], mla_heads, tq=s)
    xl, hf_l = _res(a_l.reshape(b * s, d), w_o1, 0, xl, lat_mod(1, 2), norm_ffn_g[1], lat_mod(1, 3), lat_mod(1, 4), tm=512)
    act_l = _swiglu_in(hf_l, ffn_w_in, 1)
    (out,) = _res(act_l, w_ffn_out, 1, xl, lat_mod(1, 5), final_norm_g, out_x=False, tm=512)
    return out.reshape(b, s, d)
```

```python
import functools
import math
from typing import NamedTuple

import jax
import jax.numpy as jnp
from jax import lax
from jax.experimental import pallas as pl
from jax.experimental.pallas import tpu as pltpu

F32 = jnp.float32
BF16 = jnp.bfloat16

GRID_W = 64
NORM_EPS = 1e-6
ROPE_BASE = 10000.0
DA_HEAD_DIM = 128
DA_V_DIM = 2 * DA_HEAD_DIM
DA_SCALE = DA_HEAD_DIM ** -0.5
MLA_NOPE = 128
MLA_ROPE = 64
MLA_V = 128
MLA_Q_LORA = 512
MLA_KV_LORA = 512
MLA_SCALE = (MLA_NOPE + MLA_ROPE) ** -0.5
MLA_QK_PAD = 256
LOG2_E = math.log2(math.e)
DA_Q_SCALE = DA_SCALE * LOG2_E
MLA_Q_SCALE = MLA_SCALE * LOG2_E
ATTN_SUB_ROWS = 256
MM_SUB_ROWS = 256
MM_SUB_COLS = 256
RES_SUB_ROWS = 256
Q_UP_SUB_COLS = 1024

LANES = 128
V7X_VMEM_BYTES = 64 * 1024 * 1024
VMEM_LIMIT = V7X_VMEM_BYTES - 8 * 1024 * 1024

MOD_ROWS = 16


def _params(*sem):
    return pltpu.CompilerParams(dimension_semantics=sem, vmem_limit_bytes=VMEM_LIMIT)


def _rms(x, g):
    ms = jnp.mean(x * x, axis=-1, keepdims=True)
    return x * lax.rsqrt(ms + NORM_EPS) * g


def _rot_half(x):
    return pltpu.roll(x, LANES // 2, 1)


def _mod_kernel(c_ref, w_ref, b_ref, o_ref):
    c = c_ref[...]
    s = c * jax.nn.sigmoid(c)
    o_ref[0] = jnp.dot(s.astype(BF16), w_ref[0].astype(BF16), preferred_element_type=F32) + b_ref[0]


def _mods(cond, ada_w, ada_b, tn=1024):
    depth, d, n = ada_w.shape
    return pl.pallas_call(
        _mod_kernel,
        grid=(depth, n // tn),
        in_specs=[pl.BlockSpec((MOD_ROWS, d), lambda l, j: (0, 0)),
                  pl.BlockSpec((1, d, tn), lambda l, j: (l, 0, j)),
                  pl.BlockSpec((1, 1, tn), lambda l, j: (l, 0, j))],
        out_specs=pl.BlockSpec((1, MOD_ROWS, tn), lambda l, j: (l, 0, j)),
        out_shape=jax.ShapeDtypeStruct((depth, MOD_ROWS, n), F32),
        compiler_params=_params("arbitrary", "arbitrary"),
        name="adaln_mod",
    )(cond, ada_w, ada_b.reshape(depth, 1, n))


def _norm_mod_kernel(x_ref, g_ref, shift_ref, scale_ref, o_ref):
    y = _rms(x_ref[...], g_ref[...])
    o_ref[...] = (y * (1 + scale_ref[...]) + shift_ref[...]).astype(BF16)


class _Mod(NamedTuple):
    table: jax.Array
    layer: int
    k: int
    row0: int
    groups: int


def _group_spec(m, t, tm):
    tiles_per_group = t // m.groups // tm
    d = m.table.shape[-1]
    return pl.BlockSpec((None, None, None, 1, d),
                        lambda i, *_: (m.layer, m.row0 + i // tiles_per_group, m.k, 0, 0))


def _norm_mod(x, g, shift, scale, tm=1024):
    t, d = x.shape
    return pl.pallas_call(
        _norm_mod_kernel,
        grid=(t // tm,),
        in_specs=[pl.BlockSpec((tm, d), lambda i: (i, 0)),
                  pl.BlockSpec((1, d), lambda i: (0, 0)),
                  _group_spec(shift, t, tm), _group_spec(scale, t, tm)],
        out_specs=pl.BlockSpec((tm, d), lambda i: (i, 0)),
        out_shape=jax.ShapeDtypeStruct((t, d), BF16),
        compiler_params=_params("arbitrary"),
        name="norm_mod",
    )(x, g.reshape(1, d), shift.table, scale.table)


def _qkv_rope_kernel(a_ref, w_ref, cos_ref, sin_ref, o_ref, w_bf):
    @pl.when(pl.program_id(1) == 0)
    def _():
        w_bf[...] = w_ref[...].astype(BF16)

    for r in range(a_ref.shape[0] // MM_SUB_ROWS):
        rows = slice(r * MM_SUB_ROWS, (r + 1) * MM_SUB_ROWS)
        acc = jnp.dot(a_ref[rows, :], w_bf[...], preferred_element_type=F32)
        cos = cos_ref[rows, :]
        sin = sin_ref[rows, :]
        for g in range(acc.shape[1] // LANES):
            sl = slice(g * LANES, (g + 1) * LANES)
            xg = acc[:, sl]
            o_ref[rows, sl] = (xg * cos + _rot_half(xg) * sin).astype(BF16)


def _qkv_rope(a, w, layer, cos_tab, sin_tab, tm=1024, tn=1024):
    t, k = a.shape
    n = w.shape[2]
    tiles_per_kind = n // 3 // tn
    row_tiles = cos_tab.shape[1] // tm
    tab_spec = pl.BlockSpec((None, tm, LANES), lambda j, i: (j // tiles_per_kind, i % row_tiles, 0))
    return pl.pallas_call(
        _qkv_rope_kernel,
        grid=(n // tn, t // tm),
        in_specs=[pl.BlockSpec((tm, k), lambda j, i: (i, 0)),
                  pl.BlockSpec((None, k, tn), lambda j, i: (layer, 0, j)),
                  tab_spec, tab_spec],
        out_specs=pl.BlockSpec((tm, tn), lambda j, i: (i, j)),
        out_shape=jax.ShapeDtypeStruct((t, n), BF16),
        scratch_shapes=[pltpu.VMEM((k, tn), BF16)],
        compiler_params=_params("arbitrary", "arbitrary"),
        name="qkv_rope",
    )(a, w, cos_tab, sin_tab)


def _swiglu_kernel(a_ref, wg_ref, wu_ref, o_ref, wg_bf, wu_bf):
    @pl.when(pl.program_id(1) == 0)
    def _():
        wg_bf[...] = wg_ref[...].astype(BF16)
        wu_bf[...] = wu_ref[...].astype(BF16)

    for r in range(a_ref.shape[0] // MM_SUB_ROWS):
        rows = slice(r * MM_SUB_ROWS, (r + 1) * MM_SUB_ROWS)
        a = a_ref[rows, :]
        for c in range(wg_bf.shape[1] // MM_SUB_COLS):
            cols = slice(c * MM_SUB_COLS, (c + 1) * MM_SUB_COLS)
            g = jnp.dot(a, wg_bf[:, cols], preferred_element_type=F32)
            u = jnp.dot(a, wu_bf[:, cols], preferred_element_type=F32)
            o_ref[rows, cols] = (g * jax.nn.sigmoid(g) * u).astype(BF16)


def _swiglu_in(a, w_in, layer, tm=2048, tn=512):
    t, k = a.shape
    hidden = w_in.shape[2] // 2
    nj = hidden // tn
    return pl.pallas_call(
        _swiglu_kernel,
        grid=(nj, t // tm),
        in_specs=[pl.BlockSpec((tm, k), lambda j, i: (i, 0)),
                  pl.BlockSpec((None, k, tn), lambda j, i: (layer, 0, j)),
                  pl.BlockSpec((None, k, tn), lambda j, i: (layer, 0, j + nj))],
        out_specs=pl.BlockSpec((tm, tn), lambda j, i: (i, j)),
        out_shape=jax.ShapeDtypeStruct((t, hidden), BF16),
        scratch_shapes=[pltpu.VMEM((k, tn), BF16), pltpu.VMEM((k, tn), BF16)],
        compiler_params=_params("arbitrary", "arbitrary"),
        name="swiglu_in",
    )(a, w_in, w_in)


def _res_kernel(out_x, next_norm, with_cast, a_ref, w_ref, x_ref, gate_ref, *refs):
    refs = list(refs)
    if next_norm == "mod":
        g_ref, shift_ref, scale_ref = refs[:3]
        outs = refs[3:]
    else:
        g_ref = refs[0]
        outs = refs[1:]
    if with_cast:
        cast_src_ref = outs.pop(0)
        cast_dst_ref = outs.pop(-1)
        cast_dst_ref[...] = cast_src_ref[...].astype(BF16)
    ox_ref = outs.pop(0) if out_x else None
    oh_ref = outs[0]
    if next_norm == "mod":
        gain = (g_ref[...] * (1 + scale_ref[...])).astype(BF16)
        shift = shift_ref[...].astype(BF16)
    for r in range(a_ref.shape[0] // RES_SUB_ROWS):
        rows = slice(r * RES_SUB_ROWS, (r + 1) * RES_SUB_ROWS)
        y = jnp.dot(a_ref[rows, :], w_ref[...], preferred_element_type=F32)
        xn = x_ref[rows, :] + gate_ref[...] * y
        if out_x:
            ox_ref[rows, :] = xn
        if next_norm == "mod":
            ms = jnp.mean(xn * xn, axis=-1, keepdims=True)
            oh_ref[rows, :] = (xn * lax.rsqrt(ms + NORM_EPS)).astype(BF16) * gain + shift
        else:
            oh_ref[rows, :] = _rms(xn, g_ref[...])


def _res(a, w, layer, x, gate, g, shift=None, scale=None, *, out_x=True, tm, cast=None):
    t, k = a.shape
    d = w.shape[2]
    next_norm = "final" if shift is None else "mod"
    row = pl.BlockSpec((tm, d), lambda i: (i, 0))
    in_specs = [pl.BlockSpec((tm, k), lambda i: (i, 0)),
                pl.BlockSpec((None, k, d), lambda i: (layer, 0, 0), pipeline_mode=pl.Buffered(1)),
                row, _group_spec(gate, t, tm), pl.BlockSpec((1, d), lambda i: (0, 0))]
    args = [a, w, x, gate.table, g.reshape(1, d)]
    if next_norm == "mod":
        in_specs += [_group_spec(shift, t, tm), _group_spec(scale, t, tm)]
        args += [shift.table, scale.table]
    out_specs, out_shape = [], []
    if out_x:
        out_specs.append(row)
        out_shape.append(jax.ShapeDtypeStruct((t, d), F32))
    out_specs.append(row)
    out_shape.append(jax.ShapeDtypeStruct((t, d), BF16 if next_norm == "mod" else F32))
    if cast is not None:
        src, src_layer = cast
        src_rows, src_cols = src.shape[1:]
        chunk = src_rows // (t // tm)
        assert chunk * (t // tm) == src_rows and chunk % 16 == 0
        in_specs.append(pl.BlockSpec((None, chunk, src_cols), lambda i: (src_layer, i, 0)))
        args.append(src)
        out_specs.append(pl.BlockSpec((chunk, src_cols), lambda i: (i, 0)))
        out_shape.append(jax.ShapeDtypeStruct((src_rows, src_cols), BF16))
    return pl.pallas_call(
        functools.partial(_res_kernel, out_x, next_norm, cast is not None),
        grid=(t // tm,),
        in_specs=in_specs, out_specs=out_specs, out_shape=out_shape,
        compiler_params=_params("arbitrary"),
        name="proj_residual",
    )(*args)


def _nt_dot(q, k):
    return lax.dot_general(q, k, (((1,), (1,)), ((), ())), preferred_element_type=F32)


def _softmax_pv(q, k_parts, v_refs):
    scores = [_nt_dot(q, k) for k in k_parts]
    mx = functools.reduce(jnp.maximum, [jnp.max(s, axis=1, keepdims=True) for s in scores])
    e = [jnp.exp2(s - mx) for s in scores]
    tot = functools.reduce(jnp.add, [jnp.sum(x, axis=1, keepdims=True) for x in e])
    o = functools.reduce(jnp.add, [jnp.dot(p.astype(BF16), v[...], preferred_element_type=F32)
                                   for p, v in zip(e, v_refs)])
    return o, 1.0 / tot


def _da_attn_kernel(nseg, lam_init, lamv_ref, g_ref, q_ref, *refs):
    k_refs, v_refs, o_ref = refs[:nseg], refs[nseg:2 * nseg], refs[2 * nseg]
    lv = lamv_ref[...]
    lam = (jnp.exp(jnp.sum(lv[0:1] * lv[1:2], axis=1, keepdims=True))
           - jnp.exp(jnp.sum(lv[2:3] * lv[3:4], axis=1, keepdims=True)) + lam_init)
    for hh in range(q_ref.shape[1] // DA_V_DIM):
        vcols = slice(hh * DA_V_DIM, (hh + 1) * DA_V_DIM)
        vs = [v.at[:, vcols] for v in v_refs]
        for t in range(q_ref.shape[0] // ATTN_SUB_ROWS):
            rows = slice(t * ATTN_SUB_ROWS, (t + 1) * ATTN_SUB_ROWS)
            o = None
            for m in range(2):
                sl = slice(hh * DA_V_DIM + m * DA_HEAD_DIM, hh * DA_V_DIM + (m + 1) * DA_HEAD_DIM)
                om, r = _softmax_pv(q_ref[rows, sl], [k[:, sl] for k in k_refs], vs)
                o = om * r if m == 0 else o - om * (lam * r)
            o_ref[rows, vcols] = (_rms(o, g_ref[...]) * (1.0 - lam_init)).astype(BF16)


def _da_attention(q_src, kv_srcs, lam_vecs, subln_g, lam_init, heads, tq, heads_per_step=1):
    b, sq, _ = q_src.shape
    hd = heads_per_step * DA_V_DIM
    groups = heads // heads_per_step
    in_specs = [pl.BlockSpec((4, DA_HEAD_DIM), lambda bi, h, qi: (0, 0)),
                pl.BlockSpec((1, DA_V_DIM), lambda bi, h, qi: (0, 0)),
                pl.BlockSpec((None, tq, hd), lambda bi, h, qi: (bi, qi, h))]
    args = [lam_vecs, subln_g.reshape(1, DA_V_DIM), q_src]
    for kind in (1, 2):
        for src in kv_srcs:
            in_specs.append(pl.BlockSpec((None, src.shape[1], hd),
                                         lambda bi, h, qi, kind=kind: (bi, 0, kind * groups + h)))
            args.append(src)
    return pl.pallas_call(
        functools.partial(_da_attn_kernel, len(kv_srcs), lam_init),
        grid=(b, groups, sq // tq),
        in_specs=in_specs,
        out_specs=pl.BlockSpec((None, tq, hd), lambda bi, h, qi: (bi, qi, h)),
        out_shape=jax.ShapeDtypeStruct((b, sq, heads * DA_V_DIM), BF16),
        compiler_params=_params("arbitrary", "arbitrary", "arbitrary"),
        name="diff_attention",
    )(*args)


def _mla_attn_kernel(nseg, q_ref, *refs):
    k_refs, v_refs, o_ref = refs[:nseg], refs[nseg:2 * nseg], refs[2 * nseg]
    for t in range(q_ref.shape[0] // ATTN_SUB_ROWS):
        rows = slice(t * ATTN_SUB_ROWS, (t + 1) * ATTN_SUB_ROWS)
        for hh in range(2):
            cols = slice(hh * MLA_QK_PAD, (hh + 1) * MLA_QK_PAD)
            o, r = _softmax_pv(q_ref[rows, cols], [k[:, cols] for k in k_refs], v_refs)
            half = slice(hh * MLA_V, (hh + 1) * MLA_V)
            o_ref[rows, half] = (o[:, half] * r).astype(BF16)


def _mla_attention(q, ks, vs, heads, tq):
    b, sq, _ = q.shape
    in_specs = [pl.BlockSpec((None, tq, 2 * MLA_QK_PAD), lambda bi, h, qi: (bi, qi, h))]
    in_specs += [pl.BlockSpec((None, k.shape[1], 2 * MLA_QK_PAD), lambda bi, h, qi: (bi, 0, h)) for k in ks]
    in_specs += [pl.BlockSpec((None, v.shape[1], 2 * MLA_V), lambda bi, h, qi: (bi, 0, h)) for v in vs]
    return pl.pallas_call(
        functools.partial(_mla_attn_kernel, len(ks)),
        grid=(b, heads // 2, sq // tq),
        in_specs=in_specs,
        out_specs=pl.BlockSpec((None, tq, 2 * MLA_V), lambda bi, h, qi: (bi, qi, h)),
        out_shape=jax.ShapeDtypeStruct((b, sq, heads * MLA_V), BF16),
        compiler_params=_params("arbitrary", "arbitrary", "arbitrary"),
        name="mla_attention",
    )(q, *ks, *vs)


def _mla_proj_kernel(with_q, a_ref, win_ref, qg_ref, kvg_ref, cos_ref, sin_ref, wk_ref, wv_ref, *refs):
    if with_q:
        wq_ref, q_ref, k_ref, v_ref = refs
    else:
        k_ref, v_ref = refs
    cos = cos_ref[...]
    sin = sin_ref[...]
    acc = jnp.dot(a_ref[...], win_ref[...], preferred_element_type=F32)
    ckv = _rms(acc[:, MLA_Q_LORA:MLA_Q_LORA + MLA_KV_LORA], kvg_ref[...]).astype(BF16)
    kr = acc[:, MLA_Q_LORA + MLA_KV_LORA:]
    kr = (kr * cos + _rot_half(kr) * sin).astype(BF16)
    v_ref[...] = jnp.dot(ckv, wv_ref[...], preferred_element_type=F32).astype(BF16)
    kn = jnp.dot(ckv, wk_ref[...], preferred_element_type=F32).astype(BF16)
    for h in range(kn.shape[1] // MLA_NOPE):
        base = h * MLA_QK_PAD
        k_ref[:, base:base + MLA_NOPE] = kn[:, h * MLA_NOPE:(h + 1) * MLA_NOPE]
        k_ref[:, base + MLA_NOPE:base + MLA_QK_PAD] = kr
    if with_q:
        cq = _rms(acc[:, :MLA_Q_LORA], qg_ref[...]).astype(BF16)
        qcos = cos * MLA_Q_SCALE
        qsin = sin * MLA_Q_SCALE
        for c in range(wq_ref.shape[1] // Q_UP_SUB_COLS):
            qa = jnp.dot(cq, wq_ref[:, c * Q_UP_SUB_COLS:(c + 1) * Q_UP_SUB_COLS], preferred_element_type=F32)
            for h in range(Q_UP_SUB_COLS // MLA_QK_PAD):
                lo = h * MLA_QK_PAD
                base = c * Q_UP_SUB_COLS + lo
                q_ref[:, base:base + MLA_NOPE] = (qa[:, lo:lo + MLA_NOPE] * MLA_Q_SCALE).astype(BF16)
                xr = qa[:, lo + MLA_NOPE:lo + MLA_QK_PAD]
                q_ref[:, base + MLA_NOPE:base + MLA_QK_PAD] = (xr * qcos + _rot_half(xr) * qsin).astype(BF16)


def _mla_proj(a, w_in, q_g, kv_g, cos_tab, sin_tab, wk, wv, wq=None, tm=512):
    t, k = a.shape
    with_q = wq is not None
    row_tiles = cos_tab.shape[0] // tm
    heads = wk.shape[1] // MLA_NOPE
    const = lambda arr: pl.BlockSpec(arr.shape, lambda i: (0, 0), pipeline_mode=pl.Buffered(1))
    tab = pl.BlockSpec((tm, LANES), lambda i: (i % row_tiles, 0))
    in_specs = [pl.BlockSpec((tm, k), lambda i: (i, 0)), const(w_in),
                pl.BlockSpec((1, MLA_Q_LORA), lambda i: (0, 0)),
                pl.BlockSpec((1, MLA_KV_LORA), lambda i: (0, 0)),
                tab, tab, const(wk), const(wv)]
    args = [a, w_in, q_g.reshape(1, -1), kv_g.reshape(1, -1), cos_tab, sin_tab, wk, wv]
    widths = [heads * MLA_QK_PAD, wv.shape[1]]
    if with_q:
        in_specs.append(const(wq))
        args.append(wq)
        widths.insert(0, wq.shape[1])
    return pl.pallas_call(
        functools.partial(_mla_proj_kernel, with_q),
        grid=(t // tm,),
        in_specs=in_specs,
        out_specs=[pl.BlockSpec((tm, w_), lambda i: (i, 0)) for w_ in widths],
        out_shape=[jax.ShapeDtypeStruct((t, w_), BF16) for w_ in widths],
        compiler_params=_params("arbitrary"),
        name="mla_proj",
    )(*args)


def _axial_angles(rows, rot_dim):
    row = jnp.repeat(jnp.arange(rows, dtype=F32), GRID_W)
    col = jnp.tile(jnp.arange(GRID_W, dtype=F32), rows)
    n_freq = rot_dim // 4
    inv_freq = ROPE_BASE ** (-jnp.arange(n_freq, dtype=F32) / n_freq)
    ang = jnp.concatenate([row[:, None] * inv_freq, col[:, None] * inv_freq], axis=-1)
    return jnp.cos(ang), jnp.sin(ang)


def _da_tables(seq, ctx_rows):
    cos, sin = _axial_angles(seq // GRID_W, DA_HEAD_DIM)
    cos = jnp.concatenate([cos, cos], axis=-1)
    sin = jnp.concatenate([-sin, sin], axis=-1)
    one, zero = jnp.ones_like(cos), jnp.zeros_like(cos)
    lat = (jnp.stack([cos * DA_Q_SCALE, cos, one]), jnp.stack([sin * DA_Q_SCALE, sin, zero]))
    one_c, zero_c = one[:ctx_rows], zero[:ctx_rows]
    ctx = (jnp.stack([one_c * DA_Q_SCALE, one_c, one_c]), jnp.stack([zero_c, zero_c, zero_c]))
    return lat, ctx


def _spread(x1, x2):
    z = jnp.zeros_like(x1)
    return jnp.concatenate([x1, z, x2, z], axis=-1)


def _mla_tables(seq, ctx_rows):
    cos, sin = _axial_angles(seq // GRID_W, MLA_ROPE)
    cos_l, sin_l = _spread(cos, cos), _spread(-sin, sin)
    cos_c, sin_c = jnp.ones_like(cos_l[:ctx_rows]), jnp.zeros_like(cos_l[:ctx_rows])
    return (cos_l, sin_l), (cos_c, sin_c)


def _mla_weight_layouts(w_in, w_q_up, w_kv_up):
    half = MLA_ROPE // 2
    lora = MLA_Q_LORA + MLA_KV_LORA
    w_in_p = jnp.concatenate([w_in[:, :lora], _spread(w_in[:, lora:lora + half], w_in[:, lora + half:])], axis=1)
    wq = w_q_up.reshape(MLA_Q_LORA, -1, MLA_NOPE + MLA_ROPE)
    wq_p = jnp.concatenate([wq[..., :MLA_NOPE],
                            _spread(wq[..., MLA_NOPE:MLA_NOPE + half], wq[..., MLA_NOPE + half:])], axis=-1)
    wkv = w_kv_up.reshape(MLA_KV_LORA, -1, MLA_NOPE + MLA_V)
    wk = wkv[..., :MLA_NOPE].reshape(MLA_KV_LORA, -1)
    wv = wkv[..., MLA_NOPE:].reshape(MLA_KV_LORA, -1)
    return w_in_p, wq_p.reshape(MLA_Q_LORA, -1), wk, wv


def kernel(x, c, ctx, c_ctx, ada_w, ada_b, norm_mix_g, norm_ffn_g, ffn_w_in, ffn_w_out, da_w_qkv, da_lambda, da_subln_g, da_w_o, mla_w_in, mla_q_norm_g, mla_w_q_up, mla_kv_norm_g, mla_w_kv_up, mla_w_o, final_norm_g):
    b, s, d = x.shape
    cl = ctx.shape[1]
    assert ada_w.shape[0] == 2 and b + 1 <= MOD_ROWS
    da_heads = d // DA_V_DIM
    mla_heads = d // MLA_V

    cond = jnp.concatenate([c, c_ctx[None], jnp.zeros((MOD_ROWS - b - 1, d), F32)], axis=0)
    mods = _mods(cond, ada_w, ada_b).reshape(ada_w.shape[0], MOD_ROWS, 6, 1, d)

    def lat_mod(i, k):
        return _Mod(mods, i, k, 0, b)

    def ctx_mod(i, k):
        return _Mod(mods, i, k, b, 1)

    bf = lambda w: w.astype(BF16)
    w_o0, w_o1 = bf(da_w_o), bf(mla_w_o)
    w_in_p, wq_p, wk, wv = _mla_weight_layouts(bf(mla_w_in[0]), bf(mla_w_q_up[0]), bf(mla_w_kv_up[0]))
    da_lat_tab, da_ctx_tab = _da_tables(s, 1024)
    mla_lat_tab, mla_ctx_tab = _mla_tables(s, 512)

    xl = x.reshape(b * s, d)
    xc = ctx.reshape(b * cl, d)

    lam_init = 0.8 - 0.6 * math.exp(-0.3 * 0)
    h_l = _norm_mod(xl, norm_mix_g[0], lat_mod(0, 0), lat_mod(0, 1))
    h_c = _norm_mod(xc, norm_mix_g[0], ctx_mod(0, 0), ctx_mod(0, 1))
    qkv_l = _qkv_rope(h_l, da_w_qkv, 0, *da_lat_tab).reshape(b, s, -1)
    qkv_c = _qkv_rope(h_c, da_w_qkv, 0, *da_ctx_tab).reshape(b, cl, -1)
    a_l = _da_attention(qkv_l, [qkv_c, qkv_l], da_lambda[0], da_subln_g[0], lam_init, da_heads, tq=s)
    a_c = _da_attention(qkv_c, [qkv_c], da_lambda[0], da_subln_g[0], lam_init, da_heads, tq=cl,
                        heads_per_step=da_heads)
    xl, hf_l, w_out0 = _res(a_l.reshape(b * s, d), w_o0, 0, xl, lat_mod(0, 2), norm_ffn_g[0], lat_mod(0, 3),
                            lat_mod(0, 4), tm=512, cast=(ffn_w_out, 0))
    xc, hf_c = _res(a_c.reshape(b * cl, d), w_o0, 0, xc, ctx_mod(0, 2), norm_ffn_g[0], ctx_mod(0, 3), ctx_mod(0, 4), tm=512)
    act_l = _swiglu_in(hf_l, ffn_w_in, 0)
    act_c = _swiglu_in(hf_c, ffn_w_in, 0)
    xl, hm_l = _res(act_l, w_out0[None], 0, xl, lat_mod(0, 5), norm_mix_g[1], lat_mod(1, 0), lat_mod(1, 1), tm=512)
    (hm_c,) = _res(act_c, w_out0[None], 0, xc, ctx_mod(0, 5), norm_mix_g[1], ctx_mod(1, 0), ctx_mod(1, 1),
                   out_x=False, tm=512)

    q_l, k_l, v_l = _mla_proj(hm_l, w_in_p, mla_q_norm_g[0], mla_kv_norm_g[0], *mla_lat_tab, wk, wv, wq_p)
    k_c, v_c = _mla_proj(hm_c, w_in_p, mla_q_norm_g[0], mla_kv_norm_g[0], *mla_ctx_tab, wk, wv)
    a_l = _mla_attention(q_l.reshape(b, s, -1),
                         [k_c.reshape(b, cl, -1), k_l.reshape(b, s, -1)],
                         [v_c.reshape(b, cl, -1), v_l.reshape(b, s, -1)], mla_heads, tq=s)
    xl, hf_l, w_out1 = _res(a_l.reshape(b * s, d), w_o1, 0, xl, lat_mod(1, 2), norm_ffn_g[1], lat_mod(1, 3),
                            lat_mod(1, 4), tm=512, cast=(ffn_w_out, 1))
    act_l = _swiglu_in(hf_l, ffn_w_in, 1)
    (out,) = _res(act_l, w_out1[None], 0, xl, lat_mod(1, 5), final_norm_g, out_x=False, tm=512)
    return out.reshape(b, s, d)
```

```python
import functools
import math
from typing import NamedTuple

import jax
import jax.numpy as jnp
from jax import lax
from jax.experimental import pallas as pl
from jax.experimental.pallas import tpu as pltpu

F32 = jnp.float32
BF16 = jnp.bfloat16

GRID_W = 64
NORM_EPS = 1e-6
ROPE_BASE = 10000.0
DA_HEAD_DIM = 128
DA_V_DIM = 2 * DA_HEAD_DIM
DA_SCALE = DA_HEAD_DIM ** -0.5
MLA_NOPE = 128
MLA_ROPE = 64
MLA_V = 128
MLA_Q_LORA = 512
MLA_KV_LORA = 512
MLA_SCALE = (MLA_NOPE + MLA_ROPE) ** -0.5
MLA_QK_PAD = 256
LOG2_E = math.log2(math.e)
DA_Q_SCALE = DA_SCALE * LOG2_E
MLA_Q_SCALE = MLA_SCALE * LOG2_E
ATTN_SUB_ROWS = 256
MM_SUB_ROWS = 256
MM_SUB_COLS = 256
RES_SUB_ROWS = 256
Q_UP_SUB_COLS = 1024

LANES = 128
V7X_VMEM_BYTES = 64 * 1024 * 1024
VMEM_LIMIT = V7X_VMEM_BYTES - 8 * 1024 * 1024

MOD_ROWS = 16


def _params(*sem):
    return pltpu.CompilerParams(dimension_semantics=sem, vmem_limit_bytes=VMEM_LIMIT)


def _rms(x, g):
    ms = jnp.mean(x * x, axis=-1, keepdims=True)
    return x * lax.rsqrt(ms + NORM_EPS) * g


def _rot_half(x):
    return pltpu.roll(x, LANES // 2, 1)


def _mod_kernel(c_ref, w_ref, b_ref, o_ref):
    c = c_ref[...]
    s = c * jax.nn.sigmoid(c)
    o_ref[0] = jnp.dot(s.astype(BF16), w_ref[0].astype(BF16), preferred_element_type=F32) + b_ref[0]


def _mods(cond, ada_w, ada_b, tn=1024):
    depth, d, n = ada_w.shape
    return pl.pallas_call(
        _mod_kernel,
        grid=(depth, n // tn),
        in_specs=[pl.BlockSpec((MOD_ROWS, d), lambda l, j: (0, 0)),
                  pl.BlockSpec((1, d, tn), lambda l, j: (l, 0, j)),
                  pl.BlockSpec((1, 1, tn), lambda l, j: (l, 0, j))],
        out_specs=pl.BlockSpec((1, MOD_ROWS, tn), lambda l, j: (l, 0, j)),
        out_shape=jax.ShapeDtypeStruct((depth, MOD_ROWS, n), F32),
        compiler_params=_params("arbitrary", "arbitrary"),
        name="adaln_mod",
    )(cond, ada_w, ada_b.reshape(depth, 1, n))


def _norm_mod_kernel(x_ref, g_ref, shift_ref, scale_ref, o_ref):
    y = _rms(x_ref[...], g_ref[...])
    o_ref[...] = (y * (1 + scale_ref[...]) + shift_ref[...]).astype(BF16)


class _Mod(NamedTuple):
    table: jax.Array
    layer: int
    k: int
    row0: int
    groups: int


def _group_spec(m, t, tm):
    tiles_per_group = t // m.groups // tm
    d = m.table.shape[-1]
    return pl.BlockSpec((None, None, None, 1, d),
                        lambda i, *_: (m.layer, m.row0 + i // tiles_per_group, m.k, 0, 0))


def _norm_mod(x, g, shift, scale, tm=1024):
    t, d = x.shape
    return pl.pallas_call(
        _norm_mod_kernel,
        grid=(t // tm,),
        in_specs=[pl.BlockSpec((tm, d), lambda i: (i, 0)),
                  pl.BlockSpec((1, d), lambda i: (0, 0)),
                  _group_spec(shift, t, tm), _group_spec(scale, t, tm)],
        out_specs=pl.BlockSpec((tm, d), lambda i: (i, 0)),
        out_shape=jax.ShapeDtypeStruct((t, d), BF16),
        compiler_params=_params("arbitrary"),
        name="norm_mod",
    )(x, g.reshape(1, d), shift.table, scale.table)


def _qkv_rope_kernel(a_ref, w_ref, cos_ref, sin_ref, o_ref, w_bf):
    @pl.when(pl.program_id(1) == 0)
    def _():
        w_bf[...] = w_ref[...].astype(BF16)

    for r in range(a_ref.shape[0] // MM_SUB_ROWS):
        rows = slice(r * MM_SUB_ROWS, (r + 1) * MM_SUB_ROWS)
        acc = jnp.dot(a_ref[rows, :], w_bf[...], preferred_element_type=F32)
        cos = cos_ref[rows, :]
        sin = sin_ref[rows, :]
        for g in range(acc.shape[1] // LANES):
            sl = slice(g * LANES, (g + 1) * LANES)
            xg = acc[:, sl]
            o_ref[rows, sl] = (xg * cos + _rot_half(xg) * sin).astype(BF16)


def _qkv_rope(a, w, layer, cos_tab, sin_tab, tm=1024, tn=1024):
    t, k = a.shape
    n = w.shape[2]
    tiles_per_kind = n // 3 // tn
    row_tiles = cos_tab.shape[1] // tm
    tab_spec = pl.BlockSpec((None, tm, LANES), lambda j, i: (j // tiles_per_kind, i % row_tiles, 0))
    return pl.pallas_call(
        _qkv_rope_kernel,
        grid=(n // tn, t // tm),
        in_specs=[pl.BlockSpec((tm, k), lambda j, i: (i, 0)),
                  pl.BlockSpec((None, k, tn), lambda j, i: (layer, 0, j)),
                  tab_spec, tab_spec],
        out_specs=pl.BlockSpec((tm, tn), lambda j, i: (i, j)),
        out_shape=jax.ShapeDtypeStruct((t, n), BF16),
        scratch_shapes=[pltpu.VMEM((k, tn), BF16)],
        compiler_params=_params("arbitrary", "arbitrary"),
        name="qkv_rope",
    )(a, w, cos_tab, sin_tab)


def _swiglu_kernel(with_cast, a_ref, wg_ref, wu_ref, *refs):
    if with_cast:
        cast_src_ref, o_ref, cast_dst_ref, wg_bf, wu_bf = refs
        cast_dst_ref[...] = cast_src_ref[...].astype(BF16)
    else:
        o_ref, wg_bf, wu_bf = refs

    @pl.when(pl.program_id(1) == 0)
    def _():
        wg_bf[...] = wg_ref[...].astype(BF16)
        wu_bf[...] = wu_ref[...].astype(BF16)

    for r in range(a_ref.shape[0] // MM_SUB_ROWS):
        rows = slice(r * MM_SUB_ROWS, (r + 1) * MM_SUB_ROWS)
        a = a_ref[rows, :]
        for c in range(wg_bf.shape[1] // MM_SUB_COLS):
            cols = slice(c * MM_SUB_COLS, (c + 1) * MM_SUB_COLS)
            g = jnp.dot(a, wg_bf[:, cols], preferred_element_type=F32)
            u = jnp.dot(a, wu_bf[:, cols], preferred_element_type=F32)
            o_ref[rows, cols] = (g * jax.nn.sigmoid(g) * u).astype(BF16)


def _swiglu_in(a, w_in, layer, tm=2048, tn=512, w_out=None):
    t, k = a.shape
    hidden = w_in.shape[2] // 2
    nj, ni = hidden // tn, t // tm
    in_specs = [pl.BlockSpec((tm, k), lambda j, i: (i, 0)),
                pl.BlockSpec((None, k, tn), lambda j, i: (layer, 0, j)),
                pl.BlockSpec((None, k, tn), lambda j, i: (layer, 0, j + nj))]
    args = [a, w_in, w_in]
    out_specs = [pl.BlockSpec((tm, tn), lambda j, i: (i, j))]
    out_shape = [jax.ShapeDtypeStruct((t, hidden), BF16)]
    if w_out is not None:
        rows, cols = w_out.shape[1:]
        chunk = rows // (nj * ni)
        assert chunk * nj * ni == rows and chunk % 16 == 0
        in_specs.append(pl.BlockSpec((None, chunk, cols), lambda j, i: (layer, j * ni + i, 0)))
        args.append(w_out)
        out_specs.append(pl.BlockSpec((chunk, cols), lambda j, i: (j * ni + i, 0)))
        out_shape.append(jax.ShapeDtypeStruct((rows, cols), BF16))
    outs = pl.pallas_call(
        functools.partial(_swiglu_kernel, w_out is not None),
        grid=(nj, ni),
        in_specs=in_specs, out_specs=out_specs, out_shape=out_shape,
        scratch_shapes=[pltpu.VMEM((k, tn), BF16), pltpu.VMEM((k, tn), BF16)],
        compiler_params=_params("arbitrary", "arbitrary"),
        name="swiglu_in",
    )(*args)
    return outs if w_out is not None else outs[0]


def _res_kernel(out_x, next_norm, a_ref, w_ref, x_ref, gate_ref, *refs):
    refs = list(refs)
    if next_norm == "mod":
        g_ref, shift_ref, scale_ref = refs[:3]
        outs = refs[3:]
    else:
        g_ref = refs[0]
        outs = refs[1:]
    ox_ref = outs.pop(0) if out_x else None
    oh_ref = outs[0]
    if next_norm == "mod":
        gain = (g_ref[...] * (1 + scale_ref[...])).astype(BF16)
        shift = shift_ref[...].astype(BF16)
    for r in range(a_ref.shape[0] // RES_SUB_ROWS):
        rows = slice(r * RES_SUB_ROWS, (r + 1) * RES_SUB_ROWS)
        y = jnp.dot(a_ref[rows, :], w_ref[...], preferred_element_type=F32)
        xn = x_ref[rows, :] + gate_ref[...] * y
        if out_x:
            ox_ref[rows, :] = xn
        if next_norm == "mod":
            ms = jnp.mean(xn * xn, axis=-1, keepdims=True)
            oh_ref[rows, :] = (xn * lax.rsqrt(ms + NORM_EPS)).astype(BF16) * gain + shift
        else:
            oh_ref[rows, :] = _rms(xn, g_ref[...])


def _res(a, w, layer, x, gate, g, shift=None, scale=None, *, out_x=True, tm):
    t, k = a.shape
    d = w.shape[2]
    next_norm = "final" if shift is None else "mod"
    row = pl.BlockSpec((tm, d), lambda i: (i, 0))
    in_specs = [pl.BlockSpec((tm, k), lambda i: (i, 0)),
                pl.BlockSpec((None, k, d), lambda i: (layer, 0, 0), pipeline_mode=pl.Buffered(1)),
                row, _group_spec(gate, t, tm), pl.BlockSpec((1, d), lambda i: (0, 0))]
    args = [a, w, x, gate.table, g.reshape(1, d)]
    if next_norm == "mod":
        in_specs += [_group_spec(shift, t, tm), _group_spec(scale, t, tm)]
        args += [shift.table, scale.table]
    out_specs, out_shape = [], []
    if out_x:
        out_specs.append(row)
        out_shape.append(jax.ShapeDtypeStruct((t, d), F32))
    out_specs.append(row)
    out_shape.append(jax.ShapeDtypeStruct((t, d), BF16 if next_norm == "mod" else F32))
    return pl.pallas_call(
        functools.partial(_res_kernel, out_x, next_norm),
        grid=(t // tm,),
        in_specs=in_specs, out_specs=out_specs, out_shape=out_shape,
        compiler_params=_params("arbitrary"),
        name="proj_residual",
    )(*args)


def _nt_dot(q, k):
    return lax.dot_general(q, k, (((1,), (1,)), ((), ())), preferred_element_type=F32)


def _softmax_pv(q, k_parts, v_refs):
    scores = [_nt_dot(q, k) for k in k_parts]
    mx = functools.reduce(jnp.maximum, [jnp.max(s, axis=1, keepdims=True) for s in scores])
    e = [jnp.exp2(s - mx) for s in scores]
    tot = functools.reduce(jnp.add, [jnp.sum(x, axis=1, keepdims=True) for x in e])
    o = functools.reduce(jnp.add, [jnp.dot(p.astype(BF16), v[...], preferred_element_type=F32)
                                   for p, v in zip(e, v_refs)])
    return o, 1.0 / tot


def _da_attn_kernel(nseg, lam_init, lamv_ref, g_ref, q_ref, *refs):
    k_refs, v_refs, o_ref = refs[:nseg], refs[nseg:2 * nseg], refs[2 * nseg]
    lv = lamv_ref[...]
    lam = (jnp.exp(jnp.sum(lv[0:1] * lv[1:2], axis=1, keepdims=True))
           - jnp.exp(jnp.sum(lv[2:3] * lv[3:4], axis=1, keepdims=True)) + lam_init)
    for hh in range(q_ref.shape[1] // DA_V_DIM):
        vcols = slice(hh * DA_V_DIM, (hh + 1) * DA_V_DIM)
        vs = [v.at[:, vcols] for v in v_refs]
        for t in range(q_ref.shape[0] // ATTN_SUB_ROWS):
            rows = slice(t * ATTN_SUB_ROWS, (t + 1) * ATTN_SUB_ROWS)
            o = None
            for m in range(2):
                sl = slice(hh * DA_V_DIM + m * DA_HEAD_DIM, hh * DA_V_DIM + (m + 1) * DA_HEAD_DIM)
                om, r = _softmax_pv(q_ref[rows, sl], [k[:, sl] for k in k_refs], vs)
                o = om * r if m == 0 else o - om * (lam * r)
            o_ref[rows, vcols] = (_rms(o, g_ref[...]) * (1.0 - lam_init)).astype(BF16)


def _da_attention(q_src, kv_srcs, lam_vecs, subln_g, lam_init, heads, tq, heads_per_step=1):
    b, sq, _ = q_src.shape
    hd = heads_per_step * DA_V_DIM
    groups = heads // heads_per_step
    in_specs = [pl.BlockSpec((4, DA_HEAD_DIM), lambda bi, h, qi: (0, 0)),
                pl.BlockSpec((1, DA_V_DIM), lambda bi, h, qi: (0, 0)),
                pl.BlockSpec((None, tq, hd), lambda bi, h, qi: (bi, qi, h))]
    args = [lam_vecs, subln_g.reshape(1, DA_V_DIM), q_src]
    for kind in (1, 2):
        for src in kv_srcs:
            in_specs.append(pl.BlockSpec((None, src.shape[1], hd),
                                         lambda bi, h, qi, kind=kind: (bi, 0, kind * groups + h)))
            args.append(src)
    return pl.pallas_call(
        functools.partial(_da_attn_kernel, len(kv_srcs), lam_init),
        grid=(b, groups, sq // tq),
        in_specs=in_specs,
        out_specs=pl.BlockSpec((None, tq, hd), lambda bi, h, qi: (bi, qi, h)),
        out_shape=jax.ShapeDtypeStruct((b, sq, heads * DA_V_DIM), BF16),
        compiler_params=_params("arbitrary", "arbitrary", "arbitrary"),
        name="diff_attention",
    )(*args)


def _mla_attn_kernel(nseg, q_ref, *refs):
    k_refs, v_refs, o_ref = refs[:nseg], refs[nseg:2 * nseg], refs[2 * nseg]
    for t in range(q_ref.shape[0] // ATTN_SUB_ROWS):
        rows = slice(t * ATTN_SUB_ROWS, (t + 1) * ATTN_SUB_ROWS)
        for hh in range(2):
            cols = slice(hh * MLA_QK_PAD, (hh + 1) * MLA_QK_PAD)
            o, r = _softmax_pv(q_ref[rows, cols], [k[:, cols] for k in k_refs], v_refs)
            half = slice(hh * MLA_V, (hh + 1) * MLA_V)
            o_ref[rows, half] = (o[:, half] * r).astype(BF16)


def _mla_attention(q, ks, vs, heads, tq):
    b, sq, _ = q.shape
    in_specs = [pl.BlockSpec((None, tq, 2 * MLA_QK_PAD), lambda bi, h, qi: (bi, qi, h))]
    in_specs += [pl.BlockSpec((None, k.shape[1], 2 * MLA_QK_PAD), lambda bi, h, qi: (bi, 0, h)) for k in ks]
    in_specs += [pl.BlockSpec((None, v.shape[1], 2 * MLA_V), lambda bi, h, qi: (bi, 0, h)) for v in vs]
    return pl.pallas_call(
        functools.partial(_mla_attn_kernel, len(ks)),
        grid=(b, heads // 2, sq // tq),
        in_specs=in_specs,
        out_specs=pl.BlockSpec((None, tq, 2 * MLA_V), lambda bi, h, qi: (bi, qi, h)),
        out_shape=jax.ShapeDtypeStruct((b, sq, heads * MLA_V), BF16),
        compiler_params=_params("arbitrary", "arbitrary", "arbitrary"),
        name="mla_attention",
    )(q, *ks, *vs)


def _mla_proj_kernel(with_q, a_ref, win_ref, qg_ref, kvg_ref, cos_ref, sin_ref, wk_ref, wv_ref, *refs):
    if with_q:
        wq_ref, q_ref, k_ref, v_ref = refs
    else:
        k_ref, v_ref = refs
    cos = cos_ref[...]
    sin = sin_ref[...]
    acc = jnp.dot(a_ref[...], win_ref[...], preferred_element_type=F32)
    ckv = _rms(acc[:, MLA_Q_LORA:MLA_Q_LORA + MLA_KV_LORA], kvg_ref[...]).astype(BF16)
    kr = acc[:, MLA_Q_LORA + MLA_KV_LORA:]
    kr = (kr * cos + _rot_half(kr) * sin).astype(BF16)
    v_ref[...] = jnp.dot(ckv, wv_ref[...], preferred_element_type=F32).astype(BF16)
    kn = jnp.dot(ckv, wk_ref[...], preferred_element_type=F32).astype(BF16)
    for h in range(kn.shape[1] // MLA_NOPE):
        base = h * MLA_QK_PAD
        k_ref[:, base:base + MLA_NOPE] = kn[:, h * MLA_NOPE:(h + 1) * MLA_NOPE]
        k_ref[:, base + MLA_NOPE:base + MLA_QK_PAD] = kr
    if with_q:
        cq = _rms(acc[:, :MLA_Q_LORA], qg_ref[...]).astype(BF16)
        qcos = cos * MLA_Q_SCALE
        qsin = sin * MLA_Q_SCALE
        for c in range(wq_ref.shape[1] // Q_UP_SUB_COLS):
            qa = jnp.dot(cq, wq_ref[:, c * Q_UP_SUB_COLS:(c + 1) * Q_UP_SUB_COLS], preferred_element_type=F32)
            for h in range(Q_UP_SUB_COLS // MLA_QK_PAD):
                lo = h * MLA_QK_PAD
                base = c * Q_UP_SUB_COLS + lo
                q_ref[:, base:base + MLA_NOPE] = (qa[:, lo:lo + MLA_NOPE] * MLA_Q_SCALE).astype(BF16)
                xr = qa[:, lo + MLA_NOPE:lo + MLA_QK_PAD]
                q_ref[:, base + MLA_NOPE:base + MLA_QK_PAD] = (xr * qcos + _rot_half(xr) * qsin).astype(BF16)


def _mla_proj(a, w_in, q_g, kv_g, cos_tab, sin_tab, wk, wv, wq=None, tm=512):
    t, k = a.shape
    with_q = wq is not None
    row_tiles = cos_tab.shape[0] // tm
    heads = wk.shape[1] // MLA_NOPE
    const = lambda arr: pl.BlockSpec(arr.shape, lambda i: (0, 0), pipeline_mode=pl.Buffered(1))
    tab = pl.BlockSpec((tm, LANES), lambda i: (i % row_tiles, 0))
    in_specs = [pl.BlockSpec((tm, k), lambda i: (i, 0)), const(w_in),
                pl.BlockSpec((1, MLA_Q_LORA), lambda i: (0, 0)),
                pl.BlockSpec((1, MLA_KV_LORA), lambda i: (0, 0)),
                tab, tab, const(wk), const(wv)]
    args = [a, w_in, q_g.reshape(1, -1), kv_g.reshape(1, -1), cos_tab, sin_tab, wk, wv]
    widths = [heads * MLA_QK_PAD, wv.shape[1]]
    if with_q:
        in_specs.append(const(wq))
        args.append(wq)
        widths.insert(0, wq.shape[1])
    return pl.pallas_call(
        functools.partial(_mla_proj_kernel, with_q),
        grid=(t // tm,),
        in_specs=in_specs,
        out_specs=[pl.BlockSpec((tm, w_), lambda i: (i, 0)) for w_ in widths],
        out_shape=[jax.ShapeDtypeStruct((t, w_), BF16) for w_ in widths],
        compiler_params=_params("arbitrary"),
        name="mla_proj",
    )(*args)


def _axial_angles(rows, rot_dim):
    row = jnp.repeat(jnp.arange(rows, dtype=F32), GRID_W)
    col = jnp.tile(jnp.arange(GRID_W, dtype=F32), rows)
    n_freq = rot_dim // 4
    inv_freq = ROPE_BASE ** (-jnp.arange(n_freq, dtype=F32) / n_freq)
    ang = jnp.concatenate([row[:, None] * inv_freq, col[:, None] * inv_freq], axis=-1)
    return jnp.cos(ang), jnp.sin(ang)


def _da_tables(seq, ctx_rows):
    cos, sin = _axial_angles(seq // GRID_W, DA_HEAD_DIM)
    cos = jnp.concatenate([cos, cos], axis=-1)
    sin = jnp.concatenate([-sin, sin], axis=-1)
    one, zero = jnp.ones_like(cos), jnp.zeros_like(cos)
    lat = (jnp.stack([cos * DA_Q_SCALE, cos, one]), jnp.stack([sin * DA_Q_SCALE, sin, zero]))
    one_c, zero_c = one[:ctx_rows], zero[:ctx_rows]
    ctx = (jnp.stack([one_c * DA_Q_SCALE, one_c, one_c]), jnp.stack([zero_c, zero_c, zero_c]))
    return lat, ctx


def _spread(x1, x2):
    z = jnp.zeros_like(x1)
    return jnp.concatenate([x1, z, x2, z], axis=-1)


def _mla_tables(seq, ctx_rows):
    cos, sin = _axial_angles(seq // GRID_W, MLA_ROPE)
    cos_l, sin_l = _spread(cos, cos), _spread(-sin, sin)
    cos_c, sin_c = jnp.ones_like(cos_l[:ctx_rows]), jnp.zeros_like(cos_l[:ctx_rows])
    return (cos_l, sin_l), (cos_c, sin_c)


def _mla_weight_layouts(w_in, w_q_up, w_kv_up):
    half = MLA_ROPE // 2
    lora = MLA_Q_LORA + MLA_KV_LORA
    w_in_p = jnp.concatenate([w_in[:, :lora], _spread(w_in[:, lora:lora + half], w_in[:, lora + half:])], axis=1)
    wq = w_q_up.reshape(MLA_Q_LORA, -1, MLA_NOPE + MLA_ROPE)
    wq_p = jnp.concatenate([wq[..., :MLA_NOPE],
                            _spread(wq[..., MLA_NOPE:MLA_NOPE + half], wq[..., MLA_NOPE + half:])], axis=-1)
    wkv = w_kv_up.reshape(MLA_KV_LORA, -1, MLA_NOPE + MLA_V)
    wk = wkv[..., :MLA_NOPE].reshape(MLA_KV_LORA, -1)
    wv = wkv[..., MLA_NOPE:].reshape(MLA_KV_LORA, -1)
    return w_in_p, wq_p.reshape(MLA_Q_LORA, -1), wk, wv


def kernel(x, c, ctx, c_ctx, ada_w, ada_b, norm_mix_g, norm_ffn_g, ffn_w_in, ffn_w_out, da_w_qkv, da_lambda, da_subln_g, da_w_o, mla_w_in, mla_q_norm_g, mla_w_q_up, mla_kv_norm_g, mla_w_kv_up, mla_w_o, final_norm_g):
    b, s, d = x.shape
    cl = ctx.shape[1]
    assert ada_w.shape[0] == 2 and b + 1 <= MOD_ROWS
    da_heads = d // DA_V_DIM
    mla_heads = d // MLA_V

    cond = jnp.concatenate([c, c_ctx[None], jnp.zeros((MOD_ROWS - b - 1, d), F32)], axis=0)
    mods = _mods(cond, ada_w, ada_b).reshape(ada_w.shape[0], MOD_ROWS, 6, 1, d)

    def lat_mod(i, k):
        return _Mod(mods, i, k, 0, b)

    def ctx_mod(i, k):
        return _Mod(mods, i, k, b, 1)

    bf = lambda w: w.astype(BF16)
    w_o0, w_o1 = bf(da_w_o), bf(mla_w_o)
    w_in_p, wq_p, wk, wv = _mla_weight_layouts(bf(mla_w_in[0]), bf(mla_w_q_up[0]), bf(mla_w_kv_up[0]))
    da_lat_tab, da_ctx_tab = _da_tables(s, 1024)
    mla_lat_tab, mla_ctx_tab = _mla_tables(s, 512)

    xl = x.reshape(b * s, d)
    xc = ctx.reshape(b * cl, d)

    lam_init = 0.8 - 0.6 * math.exp(-0.3 * 0)
    h_l = _norm_mod(xl, norm_mix_g[0], lat_mod(0, 0), lat_mod(0, 1))
    h_c = _norm_mod(xc, norm_mix_g[0], ctx_mod(0, 0), ctx_mod(0, 1))
    qkv_l = _qkv_rope(h_l, da_w_qkv, 0, *da_lat_tab).reshape(b, s, -1)
    qkv_c = _qkv_rope(h_c, da_w_qkv, 0, *da_ctx_tab).reshape(b, cl, -1)
    a_l = _da_attention(qkv_l, [qkv_c, qkv_l], da_lambda[0], da_subln_g[0], lam_init, da_heads, tq=s)
    a_c = _da_attention(qkv_c, [qkv_c], da_lambda[0], da_subln_g[0], lam_init, da_heads, tq=cl,
                        heads_per_step=da_heads)
    xl, hf_l = _res(a_l.reshape(b * s, d), w_o0, 0, xl, lat_mod(0, 2), norm_ffn_g[0], lat_mod(0, 3), lat_mod(0, 4), tm=512)
    xc, hf_c = _res(a_c.reshape(b * cl, d), w_o0, 0, xc, ctx_mod(0, 2), norm_ffn_g[0], ctx_mod(0, 3), ctx_mod(0, 4), tm=512)
    act_l, w_out0 = _swiglu_in(hf_l, ffn_w_in, 0, w_out=ffn_w_out)
    act_c = _swiglu_in(hf_c, ffn_w_in, 0)
    xl, hm_l = _res(act_l, w_out0[None], 0, xl, lat_mod(0, 5), norm_mix_g[1], lat_mod(1, 0), lat_mod(1, 1), tm=512)
    (hm_c,) = _res(act_c, w_out0[None], 0, xc, ctx_mod(0, 5), norm_mix_g[1], ctx_mod(1, 0), ctx_mod(1, 1),
                   out_x=False, tm=512)

    q_l, k_l, v_l = _mla_proj(hm_l, w_in_p, mla_q_norm_g[0], mla_kv_norm_g[0], *mla_lat_tab, wk, wv, wq_p)
    k_c, v_c = _mla_proj(hm_c, w_in_p, mla_q_norm_g[0], mla_kv_norm_g[0], *mla_ctx_tab, wk, wv)
    a_l = _mla_attention(q_l.reshape(b, s, -1),
                         [k_c.reshape(b, cl, -1), k_l.reshape(b, s, -1)],
                         [v_c.reshape(b, cl, -1), v_l.reshape(b, s, -1)], mla_heads, tq=s)
    xl, hf_l = _res(a_l.reshape(b * s, d), w_o1, 0, xl, lat_mod(1, 2), norm_ffn_g[1], lat_mod(1, 3), lat_mod(1, 4), tm=512)
    act_l, w_out1 = _swiglu_in(hf_l, ffn_w_in, 1, w_out=ffn_w_out)
    (out,) = _res(act_l, w_out1[None], 0, xl, lat_mod(1, 5), final_norm_g, out_x=False, tm=512)
    return out.reshape(b, s, d)
```

```python
import functools
import math
from typing import NamedTuple

import jax
import jax.numpy as jnp
from jax import lax
from jax.experimental import pallas as pl
from jax.experimental.pallas import tpu as pltpu

F32 = jnp.float32
BF16 = jnp.bfloat16

GRID_W = 64
NORM_EPS = 1e-6
ROPE_BASE = 10000.0
DA_HEAD_DIM = 128
DA_V_DIM = 2 * DA_HEAD_DIM
DA_SCALE = DA_HEAD_DIM ** -0.5
MLA_NOPE = 128
MLA_ROPE = 64
MLA_V = 128
MLA_Q_LORA = 512
MLA_KV_LORA = 512
MLA_SCALE = (MLA_NOPE + MLA_ROPE) ** -0.5
MLA_QK_PAD = 256
LOG2_E = math.log2(math.e)
DA_Q_SCALE = DA_SCALE * LOG2_E
MLA_Q_SCALE = MLA_SCALE * LOG2_E
ATTN_SUB_ROWS = 256
MM_SUB_ROWS = 256
MM_SUB_COLS = 256
RES_SUB_ROWS = 256
Q_UP_SUB_COLS = 1024

LANES = 128
V7X_VMEM_BYTES = 64 * 1024 * 1024
VMEM_LIMIT = V7X_VMEM_BYTES - 8 * 1024 * 1024

MOD_ROWS = 16


def _params(*sem):
    return pltpu.CompilerParams(dimension_semantics=sem, vmem_limit_bytes=VMEM_LIMIT)


def _rms(x, g):
    ms = jnp.mean(x * x, axis=-1, keepdims=True)
    return x * lax.rsqrt(ms + NORM_EPS) * g


def _rot_half(x):
    return pltpu.roll(x, LANES // 2, 1)


def _mod_kernel(c_ref, w_ref, b_ref, o_ref):
    c = c_ref[...]
    s = c * jax.nn.sigmoid(c)
    o_ref[0] = jnp.dot(s.astype(BF16), w_ref[0].astype(BF16), preferred_element_type=F32) + b_ref[0]


def _mods(cond, ada_w, ada_b, tn=1024):
    depth, d, n = ada_w.shape
    return pl.pallas_call(
        _mod_kernel,
        grid=(depth, n // tn),
        in_specs=[pl.BlockSpec((MOD_ROWS, d), lambda l, j: (0, 0)),
                  pl.BlockSpec((1, d, tn), lambda l, j: (l, 0, j)),
                  pl.BlockSpec((1, 1, tn), lambda l, j: (l, 0, j))],
        out_specs=pl.BlockSpec((1, MOD_ROWS, tn), lambda l, j: (l, 0, j)),
        out_shape=jax.ShapeDtypeStruct((depth, MOD_ROWS, n), F32),
        compiler_params=_params("arbitrary", "arbitrary"),
        name="adaln_mod",
    )(cond, ada_w, ada_b.reshape(depth, 1, n))


def _norm_mod_kernel(x_ref, g_ref, shift_ref, scale_ref, o_ref):
    y = _rms(x_ref[...], g_ref[...])
    o_ref[...] = (y * (1 + scale_ref[...]) + shift_ref[...]).astype(BF16)


class _Mod(NamedTuple):
    table: jax.Array
    layer: int
    k: int
    row0: int
    groups: int


def _group_spec(m, t, tm):
    tiles_per_group = t // m.groups // tm
    d = m.table.shape[-1]
    return pl.BlockSpec((None, None, None, 1, d),
                        lambda i, *_: (m.layer, m.row0 + i // tiles_per_group, m.k, 0, 0))


def _norm_mod(x, g, shift, scale, tm=1024):
    t, d = x.shape
    return pl.pallas_call(
        _norm_mod_kernel,
        grid=(t // tm,),
        in_specs=[pl.BlockSpec((tm, d), lambda i: (i, 0)),
                  pl.BlockSpec((1, d), lambda i: (0, 0)),
                  _group_spec(shift, t, tm), _group_spec(scale, t, tm)],
        out_specs=pl.BlockSpec((tm, d), lambda i: (i, 0)),
        out_shape=jax.ShapeDtypeStruct((t, d), BF16),
        compiler_params=_params("arbitrary"),
        name="norm_mod",
    )(x, g.reshape(1, d), shift.table, scale.table)


def _qkv_rope_kernel(a_ref, w_ref, cos_ref, sin_ref, o_ref, w_bf):
    @pl.when(pl.program_id(1) == 0)
    def _():
        w_bf[...] = w_ref[...].astype(BF16)

    for r in range(a_ref.shape[0] // MM_SUB_ROWS):
        rows = slice(r * MM_SUB_ROWS, (r + 1) * MM_SUB_ROWS)
        acc = jnp.dot(a_ref[rows, :], w_bf[...], preferred_element_type=F32)
        cos = cos_ref[rows, :]
        sin = sin_ref[rows, :]
        for g in range(acc.shape[1] // LANES):
            sl = slice(g * LANES, (g + 1) * LANES)
            xg = acc[:, sl]
            o_ref[rows, sl] = (xg * cos + _rot_half(xg) * sin).astype(BF16)


def _qkv_rope(a, w, layer, cos_tab, sin_tab, tm=1024, tn=1024):
    t, k = a.shape
    n = w.shape[2]
    tiles_per_kind = n // 3 // tn
    row_tiles = cos_tab.shape[1] // tm
    tab_spec = pl.BlockSpec((None, tm, LANES), lambda j, i: (j // tiles_per_kind, i % row_tiles, 0))
    return pl.pallas_call(
        _qkv_rope_kernel,
        grid=(n // tn, t // tm),
        in_specs=[pl.BlockSpec((tm, k), lambda j, i: (i, 0)),
                  pl.BlockSpec((None, k, tn), lambda j, i: (layer, 0, j)),
                  tab_spec, tab_spec],
        out_specs=pl.BlockSpec((tm, tn), lambda j, i: (i, j)),
        out_shape=jax.ShapeDtypeStruct((t, n), BF16),
        scratch_shapes=[pltpu.VMEM((k, tn), BF16)],
        compiler_params=_params("arbitrary", "arbitrary"),
        name="qkv_rope",
    )(a, w, cos_tab, sin_tab)


def _swiglu_kernel(with_cast, a_ref, wg_ref, wu_ref, *refs):
    if with_cast:
        cast_src_ref, o_ref, cast_dst_ref, wg_bf, wu_bf = refs
        cast_dst_ref[...] = cast_src_ref[...].astype(BF16)
    else:
        o_ref, wg_bf, wu_bf = refs

    @pl.when(pl.program_id(1) == 0)
    def _():
        wg_bf[...] = wg_ref[...].astype(BF16)
        wu_bf[...] = wu_ref[...].astype(BF16)

    for r in range(a_ref.shape[0] // MM_SUB_ROWS):
        rows = slice(r * MM_SUB_ROWS, (r + 1) * MM_SUB_ROWS)
        a = a_ref[rows, :]
        for c in range(wg_bf.shape[1] // MM_SUB_COLS):
            cols = slice(c * MM_SUB_COLS, (c + 1) * MM_SUB_COLS)
            g = jnp.dot(a, wg_bf[:, cols], preferred_element_type=F32)
            u = jnp.dot(a, wu_bf[:, cols], preferred_element_type=F32)
            o_ref[rows, cols] = (g * jax.nn.sigmoid(g) * u).astype(BF16)


def _swiglu_in(a, w_in, layer, tm=2048, tn=512, w_out=None):
    t, k = a.shape
    hidden = w_in.shape[2] // 2
    nj, ni = hidden // tn, t // tm
    in_specs = [pl.BlockSpec((tm, k), lambda j, i: (i, 0)),
                pl.BlockSpec((None, k, tn), lambda j, i: (layer, 0, j)),
                pl.BlockSpec((None, k, tn), lambda j, i: (layer, 0, j + nj))]
    args = [a, w_in, w_in]
    out_specs = [pl.BlockSpec((tm, tn), lambda j, i: (i, j))]
    out_shape = [jax.ShapeDtypeStruct((t, hidden), BF16)]
    if w_out is not None:
        rows, cols = w_out.shape[1:]
        chunk = rows // (nj * ni)
        assert chunk * nj * ni == rows and chunk % 16 == 0
        in_specs.append(pl.BlockSpec((None, chunk, cols), lambda j, i: (layer, j * ni + i, 0)))
        args.append(w_out)
        out_specs.append(pl.BlockSpec((chunk, cols), lambda j, i: (j * ni + i, 0)))
        out_shape.append(jax.ShapeDtypeStruct((rows, cols), BF16))
    outs = pl.pallas_call(
        functools.partial(_swiglu_kernel, w_out is not None),
        grid=(nj, ni),
        in_specs=in_specs, out_specs=out_specs, out_shape=out_shape,
        scratch_shapes=[pltpu.VMEM((k, tn), BF16), pltpu.VMEM((k, tn), BF16)],
        compiler_params=_params("arbitrary", "arbitrary"),
        name="swiglu_in",
    )(*args)
    return outs if w_out is not None else outs[0]


def _res_kernel(out_x, next_norm, a_ref, w_ref, x_ref, gate_ref, *refs):
    refs = list(refs)
    if next_norm == "mod":
        g_ref, shift_ref, scale_ref = refs[:3]
        outs = refs[3:]
    else:
        g_ref = refs[0]
        outs = refs[1:]
    ox_ref = outs.pop(0) if out_x else None
    oh_ref = outs[0]
    if next_norm == "mod":
        gain = (g_ref[...] * (1 + scale_ref[...])).astype(BF16)
        shift = shift_ref[...].astype(BF16)
    for r in range(a_ref.shape[0] // RES_SUB_ROWS):
        rows = slice(r * RES_SUB_ROWS, (r + 1) * RES_SUB_ROWS)
        y = jnp.dot(a_ref[rows, :], w_ref[...], preferred_element_type=F32)
        xn = x_ref[rows, :] + gate_ref[...] * y
        if out_x:
            ox_ref[rows, :] = xn
        if next_norm == "mod":
            ms = jnp.mean(xn * xn, axis=-1, keepdims=True)
            oh_ref[rows, :] = (xn * lax.rsqrt(ms + NORM_EPS)).astype(BF16) * gain + shift
        else:
            oh_ref[rows, :] = _rms(xn, g_ref[...])


def _res(a, w, layer, x, gate, g, shift=None, scale=None, *, out_x=True, tm):
    t, k = a.shape
    d = w.shape[2]
    next_norm = "final" if shift is None else "mod"
    row = pl.BlockSpec((tm, d), lambda i: (i, 0))
    in_specs = [pl.BlockSpec((tm, k), lambda i: (i, 0)),
                pl.BlockSpec((None, k, d), lambda i: (layer, 0, 0), pipeline_mode=pl.Buffered(1)),
                row, _group_spec(gate, t, tm), pl.BlockSpec((1, d), lambda i: (0, 0))]
    args = [a, w, x, gate.table, g.reshape(1, d)]
    if next_norm == "mod":
        in_specs += [_group_spec(shift, t, tm), _group_spec(scale, t, tm)]
        args += [shift.table, scale.table]
    out_specs, out_shape = [], []
    if out_x:
        out_specs.append(row)
        out_shape.append(jax.ShapeDtypeStruct((t, d), F32))
    out_specs.append(row)
    out_shape.append(jax.ShapeDtypeStruct((t, d), BF16 if next_norm == "mod" else F32))
    return pl.pallas_call(
        functools.partial(_res_kernel, out_x, next_norm),
        grid=(t // tm,),
        in_specs=in_specs, out_specs=out_specs, out_shape=out_shape,
        compiler_params=_params("arbitrary"),
        name="proj_residual",
    )(*args)


def _nt_dot(q, k):
    return lax.dot_general(q, k, (((1,), (1,)), ((), ())), preferred_element_type=F32)


def _softmax_pv(q, k_parts, v_refs):
    scores = [_nt_dot(q, k) for k in k_parts]
    mx = functools.reduce(jnp.maximum, [jnp.max(s, axis=1, keepdims=True) for s in scores])
    e = [jnp.exp2(s - mx) for s in scores]
    tot = functools.reduce(jnp.add, [jnp.sum(x, axis=1, keepdims=True) for x in e])
    o = functools.reduce(jnp.add, [jnp.dot(p.astype(BF16), v[...], preferred_element_type=F32)
                                   for p, v in zip(e, v_refs)])
    return o, 1.0 / tot


def _side_cast_specs(w, n_steps, step_of):
    rows, cols = w.shape[1:]
    chunk = rows // n_steps
    assert chunk * n_steps == rows and chunk % 16 == 0
    return (pl.BlockSpec((None, chunk, cols), lambda *g: (0, step_of(*g), 0)),
            pl.BlockSpec((chunk, cols), lambda *g: (step_of(*g), 0)),
            jax.ShapeDtypeStruct((rows, cols), BF16))


def _split_side_cast(with_cast, refs):
    if not with_cast:
        return refs[0]
    cast_src_ref, o_ref, cast_dst_ref = refs
    cast_dst_ref[...] = cast_src_ref[...].astype(BF16)
    return o_ref


def _da_attn_kernel(nseg, lam_init, with_cast, lamv_ref, g_ref, q_ref, *refs):
    k_refs, v_refs = refs[:nseg], refs[nseg:2 * nseg]
    o_ref = _split_side_cast(with_cast, refs[2 * nseg:])
    lv = lamv_ref[...]
    lam = (jnp.exp(jnp.sum(lv[0:1] * lv[1:2], axis=1, keepdims=True))
           - jnp.exp(jnp.sum(lv[2:3] * lv[3:4], axis=1, keepdims=True)) + lam_init)
    for hh in range(q_ref.shape[1] // DA_V_DIM):
        vcols = slice(hh * DA_V_DIM, (hh + 1) * DA_V_DIM)
        vs = [v.at[:, vcols] for v in v_refs]
        for t in range(q_ref.shape[0] // ATTN_SUB_ROWS):
            rows = slice(t * ATTN_SUB_ROWS, (t + 1) * ATTN_SUB_ROWS)
            o = None
            for m in range(2):
                sl = slice(hh * DA_V_DIM + m * DA_HEAD_DIM, hh * DA_V_DIM + (m + 1) * DA_HEAD_DIM)
                om, r = _softmax_pv(q_ref[rows, sl], [k[:, sl] for k in k_refs], vs)
                o = om * r if m == 0 else o - om * (lam * r)
            o_ref[rows, vcols] = (_rms(o, g_ref[...]) * (1.0 - lam_init)).astype(BF16)


def _da_attention(q_src, kv_srcs, lam_vecs, subln_g, lam_init, heads, tq, heads_per_step=1, w_next=None):
    b, sq, _ = q_src.shape
    hd = heads_per_step * DA_V_DIM
    groups = heads // heads_per_step
    in_specs = [pl.BlockSpec((4, DA_HEAD_DIM), lambda bi, h, qi: (0, 0)),
                pl.BlockSpec((1, DA_V_DIM), lambda bi, h, qi: (0, 0)),
                pl.BlockSpec((None, tq, hd), lambda bi, h, qi: (bi, qi, h))]
    args = [lam_vecs, subln_g.reshape(1, DA_V_DIM), q_src]
    for kind in (1, 2):
        for src in kv_srcs:
            in_specs.append(pl.BlockSpec((None, src.shape[1], hd),
                                         lambda bi, h, qi, kind=kind: (bi, 0, kind * groups + h)))
            args.append(src)
    nq = sq // tq
    out_specs = [pl.BlockSpec((None, tq, hd), lambda bi, h, qi: (bi, qi, h))]
    out_shape = [jax.ShapeDtypeStruct((b, sq, heads * DA_V_DIM), BF16)]
    if w_next is not None:
        src_spec, dst_spec, dst_shape = _side_cast_specs(
            w_next, b * groups * nq, lambda bi, h, qi: (bi * groups + h) * nq + qi)
        in_specs.append(src_spec)
        args.append(w_next)
        out_specs.append(dst_spec)
        out_shape.append(dst_shape)
    outs = pl.pallas_call(
        functools.partial(_da_attn_kernel, len(kv_srcs), lam_init, w_next is not None),
        grid=(b, groups, nq),
        in_specs=in_specs, out_specs=out_specs, out_shape=out_shape,
        compiler_params=_params("arbitrary", "arbitrary", "arbitrary"),
        name="diff_attention",
    )(*args)
    return outs if w_next is not None else outs[0]


def _mla_attn_kernel(nseg, with_cast, q_ref, *refs):
    k_refs, v_refs = refs[:nseg], refs[nseg:2 * nseg]
    o_ref = _split_side_cast(with_cast, refs[2 * nseg:])
    for t in range(q_ref.shape[0] // ATTN_SUB_ROWS):
        rows = slice(t * ATTN_SUB_ROWS, (t + 1) * ATTN_SUB_ROWS)
        for hh in range(2):
            cols = slice(hh * MLA_QK_PAD, (hh + 1) * MLA_QK_PAD)
            o, r = _softmax_pv(q_ref[rows, cols], [k[:, cols] for k in k_refs], v_refs)
            half = slice(hh * MLA_V, (hh + 1) * MLA_V)
            o_ref[rows, half] = (o[:, half] * r).astype(BF16)


def _mla_attention(q, ks, vs, heads, tq, w_next=None):
    b, sq, _ = q.shape
    pairs, nq = heads // 2, sq // tq
    in_specs = [pl.BlockSpec((None, tq, 2 * MLA_QK_PAD), lambda bi, h, qi: (bi, qi, h))]
    in_specs += [pl.BlockSpec((None, k.shape[1], 2 * MLA_QK_PAD), lambda bi, h, qi: (bi, 0, h)) for k in ks]
    in_specs += [pl.BlockSpec((None, v.shape[1], 2 * MLA_V), lambda bi, h, qi: (bi, 0, h)) for v in vs]
    args = [q, *ks, *vs]
    out_specs = [pl.BlockSpec((None, tq, 2 * MLA_V), lambda bi, h, qi: (bi, qi, h))]
    out_shape = [jax.ShapeDtypeStruct((b, sq, heads * MLA_V), BF16)]
    if w_next is not None:
        src_spec, dst_spec, dst_shape = _side_cast_specs(
            w_next, b * pairs * nq, lambda bi, h, qi: (bi * pairs + h) * nq + qi)
        in_specs.append(src_spec)
        args.append(w_next)
        out_specs.append(dst_spec)
        out_shape.append(dst_shape)
    outs = pl.pallas_call(
        functools.partial(_mla_attn_kernel, len(ks), w_next is not None),
        grid=(b, pairs, nq),
        in_specs=in_specs, out_specs=out_specs, out_shape=out_shape,
        compiler_params=_params("arbitrary", "arbitrary", "arbitrary"),
        name="mla_attention",
    )(*args)
    return outs if w_next is not None else outs[0]


def _mla_proj_kernel(with_q, a_ref, win_ref, qg_ref, kvg_ref, cos_ref, sin_ref, wk_ref, wv_ref, *refs):
    if with_q:
        wq_ref, q_ref, k_ref, v_ref = refs
    else:
        k_ref, v_ref = refs
    cos = cos_ref[...]
    sin = sin_ref[...]
    acc = jnp.dot(a_ref[...], win_ref[...], preferred_element_type=F32)
    ckv = _rms(acc[:, MLA_Q_LORA:MLA_Q_LORA + MLA_KV_LORA], kvg_ref[...]).astype(BF16)
    kr = acc[:, MLA_Q_LORA + MLA_KV_LORA:]
    kr = (kr * cos + _rot_half(kr) * sin).astype(BF16)
    v_ref[...] = jnp.dot(ckv, wv_ref[...], preferred_element_type=F32).astype(BF16)
    kn = jnp.dot(ckv, wk_ref[...], preferred_element_type=F32).astype(BF16)
    for h in range(kn.shape[1] // MLA_NOPE):
        base = h * MLA_QK_PAD
        k_ref[:, base:base + MLA_NOPE] = kn[:, h * MLA_NOPE:(h + 1) * MLA_NOPE]
        k_ref[:, base + MLA_NOPE:base + MLA_QK_PAD] = kr
    if with_q:
        cq = _rms(acc[:, :MLA_Q_LORA], qg_ref[...]).astype(BF16)
        qcos = cos * MLA_Q_SCALE
        qsin = sin * MLA_Q_SCALE
        for c in range(wq_ref.shape[1] // Q_UP_SUB_COLS):
            qa = jnp.dot(cq, wq_ref[:, c * Q_UP_SUB_COLS:(c + 1) * Q_UP_SUB_COLS], preferred_element_type=F32)
            for h in range(Q_UP_SUB_COLS // MLA_QK_PAD):
                lo = h * MLA_QK_PAD
                base = c * Q_UP_SUB_COLS + lo
                q_ref[:, base:base + MLA_NOPE] = (qa[:, lo:lo + MLA_NOPE] * MLA_Q_SCALE).astype(BF16)
                xr = qa[:, lo + MLA_NOPE:lo + MLA_QK_PAD]
                q_ref[:, base + MLA_NOPE:base + MLA_QK_PAD] = (xr * qcos + _rot_half(xr) * qsin).astype(BF16)


def _mla_proj(a, w_in, q_g, kv_g, cos_tab, sin_tab, wk, wv, wq=None, tm=512):
    t, k = a.shape
    with_q = wq is not None
    row_tiles = cos_tab.shape[0] // tm
    heads = wk.shape[1] // MLA_NOPE
    const = lambda arr: pl.BlockSpec(arr.shape, lambda i: (0, 0), pipeline_mode=pl.Buffered(1))
    tab = pl.BlockSpec((tm, LANES), lambda i: (i % row_tiles, 0))
    in_specs = [pl.BlockSpec((tm, k), lambda i: (i, 0)), const(w_in),
                pl.BlockSpec((1, MLA_Q_LORA), lambda i: (0, 0)),
                pl.BlockSpec((1, MLA_KV_LORA), lambda i: (0, 0)),
                tab, tab, const(wk), const(wv)]
    args = [a, w_in, q_g.reshape(1, -1), kv_g.reshape(1, -1), cos_tab, sin_tab, wk, wv]
    widths = [heads * MLA_QK_PAD, wv.shape[1]]
    if with_q:
        in_specs.append(const(wq))
        args.append(wq)
        widths.insert(0, wq.shape[1])
    return pl.pallas_call(
        functools.partial(_mla_proj_kernel, with_q),
        grid=(t // tm,),
        in_specs=in_specs,
        out_specs=[pl.BlockSpec((tm, w_), lambda i: (i, 0)) for w_ in widths],
        out_shape=[jax.ShapeDtypeStruct((t, w_), BF16) for w_ in widths],
        compiler_params=_params("arbitrary"),
        name="mla_proj",
    )(*args)


def _axial_angles(rows, rot_dim):
    row = jnp.repeat(jnp.arange(rows, dtype=F32), GRID_W)
    col = jnp.tile(jnp.arange(GRID_W, dtype=F32), rows)
    n_freq = rot_dim // 4
    inv_freq = ROPE_BASE ** (-jnp.arange(n_freq, dtype=F32) / n_freq)
    ang = jnp.concatenate([row[:, None] * inv_freq, col[:, None] * inv_freq], axis=-1)
    return jnp.cos(ang), jnp.sin(ang)


def _da_tables(seq, ctx_rows):
    cos, sin = _axial_angles(seq // GRID_W, DA_HEAD_DIM)
    cos = jnp.concatenate([cos, cos], axis=-1)
    sin = jnp.concatenate([-sin, sin], axis=-1)
    one, zero = jnp.ones_like(cos), jnp.zeros_like(cos)
    lat = (jnp.stack([cos * DA_Q_SCALE, cos, one]), jnp.stack([sin * DA_Q_SCALE, sin, zero]))
    one_c, zero_c = one[:ctx_rows], zero[:ctx_rows]
    ctx = (jnp.stack([one_c * DA_Q_SCALE, one_c, one_c]), jnp.stack([zero_c, zero_c, zero_c]))
    return lat, ctx


def _spread(x1, x2):
    z = jnp.zeros_like(x1)
    return jnp.concatenate([x1, z, x2, z], axis=-1)


def _mla_tables(seq, ctx_rows):
    cos, sin = _axial_angles(seq // GRID_W, MLA_ROPE)
    cos_l, sin_l = _spread(cos, cos), _spread(-sin, sin)
    cos_c, sin_c = jnp.ones_like(cos_l[:ctx_rows]), jnp.zeros_like(cos_l[:ctx_rows])
    return (cos_l, sin_l), (cos_c, sin_c)


def _mla_weight_layouts(w_in, w_q_up, w_kv_up):
    half = MLA_ROPE // 2
    lora = MLA_Q_LORA + MLA_KV_LORA
    w_in_p = jnp.concatenate([w_in[:, :lora], _spread(w_in[:, lora:lora + half], w_in[:, lora + half:])], axis=1)
    wq = w_q_up.reshape(MLA_Q_LORA, -1, MLA_NOPE + MLA_ROPE)
    wq_p = jnp.concatenate([wq[..., :MLA_NOPE],
                            _spread(wq[..., MLA_NOPE:MLA_NOPE + half], wq[..., MLA_NOPE + half:])], axis=-1)
    wkv = w_kv_up.reshape(MLA_KV_LORA, -1, MLA_NOPE + MLA_V)
    wk = wkv[..., :MLA_NOPE].reshape(MLA_KV_LORA, -1)
    wv = wkv[..., MLA_NOPE:].reshape(MLA_KV_LORA, -1)
    return w_in_p, wq_p.reshape(MLA_Q_LORA, -1), wk, wv


def kernel(x, c, ctx, c_ctx, ada_w, ada_b, norm_mix_g, norm_ffn_g, ffn_w_in, ffn_w_out, da_w_qkv, da_lambda, da_subln_g, da_w_o, mla_w_in, mla_q_norm_g, mla_w_q_up, mla_kv_norm_g, mla_w_kv_up, mla_w_o, final_norm_g):
    b, s, d = x.shape
    cl = ctx.shape[1]
    assert ada_w.shape[0] == 2 and b + 1 <= MOD_ROWS
    da_heads = d // DA_V_DIM
    mla_heads = d // MLA_V

    cond = jnp.concatenate([c, c_ctx[None], jnp.zeros((MOD_ROWS - b - 1, d), F32)], axis=0)
    mods = _mods(cond, ada_w, ada_b).reshape(ada_w.shape[0], MOD_ROWS, 6, 1, d)

    def lat_mod(i, k):
        return _Mod(mods, i, k, 0, b)

    def ctx_mod(i, k):
        return _Mod(mods, i, k, b, 1)

    bf = lambda w: w.astype(BF16)
    w_in_p, wq_p, wk, wv = _mla_weight_layouts(bf(mla_w_in[0]), bf(mla_w_q_up[0]), bf(mla_w_kv_up[0]))
    da_lat_tab, da_ctx_tab = _da_tables(s, 1024)
    mla_lat_tab, mla_ctx_tab = _mla_tables(s, 512)

    xl = x.reshape(b * s, d)
    xc = ctx.reshape(b * cl, d)

    lam_init = 0.8 - 0.6 * math.exp(-0.3 * 0)
    h_l = _norm_mod(xl, norm_mix_g[0], lat_mod(0, 0), lat_mod(0, 1))
    h_c = _norm_mod(xc, norm_mix_g[0], ctx_mod(0, 0), ctx_mod(0, 1))
    qkv_l = _qkv_rope(h_l, da_w_qkv, 0, *da_lat_tab).reshape(b, s, -1)
    qkv_c = _qkv_rope(h_c, da_w_qkv, 0, *da_ctx_tab).reshape(b, cl, -1)
    a_l, w_o0 = _da_attention(qkv_l, [qkv_c, qkv_l], da_lambda[0], da_subln_g[0], lam_init, da_heads, tq=s,
                              w_next=da_w_o)
    a_c = _da_attention(qkv_c, [qkv_c], da_lambda[0], da_subln_g[0], lam_init, da_heads, tq=cl,
                        heads_per_step=da_heads)
    xl, hf_l = _res(a_l.reshape(b * s, d), w_o0[None], 0, xl, lat_mod(0, 2), norm_ffn_g[0], lat_mod(0, 3), lat_mod(0, 4), tm=512)
    xc, hf_c = _res(a_c.reshape(b * cl, d), w_o0[None], 0, xc, ctx_mod(0, 2), norm_ffn_g[0], ctx_mod(0, 3), ctx_mod(0, 4), tm=512)
    act_l, w_out0 = _swiglu_in(hf_l, ffn_w_in, 0, w_out=ffn_w_out)
    act_c = _swiglu_in(hf_c, ffn_w_in, 0)
    xl, hm_l = _res(act_l, w_out0[None], 0, xl, lat_mod(0, 5), norm_mix_g[1], lat_mod(1, 0), lat_mod(1, 1), tm=512)
    (hm_c,) = _res(act_c, w_out0[None], 0, xc, ctx_mod(0, 5), norm_mix_g[1], ctx_mod(1, 0), ctx_mod(1, 1),
                   out_x=False, tm=512)

    q_l, k_l, v_l = _mla_proj(hm_l, w_in_p, mla_q_norm_g[0], mla_kv_norm_g[0], *mla_lat_tab, wk, wv, wq_p)
    k_c, v_c = _mla_proj(hm_c, w_in_p, mla_q_norm_g[0], mla_kv_norm_g[0], *mla_ctx_tab, wk, wv)
    a_l, w_o1 = _mla_attention(q_l.reshape(b, s, -1),
                               [k_c.reshape(b, cl, -1), k_l.reshape(b, s, -1)],
                               [v_c.reshape(b, cl, -1), v_l.reshape(b, s, -1)], mla_heads, tq=s, w_next=mla_w_o)
    xl, hf_l = _res(a_l.reshape(b * s, d), w_o1[None], 0, xl, lat_mod(1, 2), norm_ffn_g[1], lat_mod(1, 3), lat_mod(1, 4), tm=512)
    act_l, w_out1 = _swiglu_in(hf_l, ffn_w_in, 1, w_out=ffn_w_out)
    (out,) = _res(act_l, w_out1[None], 0, xl, lat_mod(1, 5), final_norm_g, out_x=False, tm=512)
    return out.reshape(b, s, d)
```

```python
import functools
import math
from typing import NamedTuple

import jax
import jax.numpy as jnp
from jax import lax
from jax.experimental import pallas as pl
from jax.experimental.pallas import tpu as pltpu

F32 = jnp.float32
BF16 = jnp.bfloat16

GRID_W = 64
NORM_EPS = 1e-6
ROPE_BASE = 10000.0
DA_HEAD_DIM = 128
DA_V_DIM = 2 * DA_HEAD_DIM
DA_SCALE = DA_HEAD_DIM ** -0.5
MLA_NOPE = 128
MLA_ROPE = 64
MLA_V = 128
MLA_Q_LORA = 512
MLA_KV_LORA = 512
MLA_SCALE = (MLA_NOPE + MLA_ROPE) ** -0.5
MLA_QK_PAD = 256
LOG2_E = math.log2(math.e)
DA_Q_SCALE = DA_SCALE * LOG2_E
MLA_Q_SCALE = MLA_SCALE * LOG2_E
ATTN_SUB_ROWS = 256
MM_SUB_ROWS = 256
MM_SUB_COLS = 256
RES_SUB_ROWS = 256
Q_UP_SUB_COLS = 1024

LANES = 128
V7X_VMEM_BYTES = 64 * 1024 * 1024
VMEM_LIMIT = V7X_VMEM_BYTES - 8 * 1024 * 1024

MOD_ROWS = 16


def _params(*sem):
    return pltpu.CompilerParams(dimension_semantics=sem, vmem_limit_bytes=VMEM_LIMIT)


def _rms(x, g):
    ms = jnp.mean(x * x, axis=-1, keepdims=True)
    return x * lax.rsqrt(ms + NORM_EPS) * g


def _rot_half(x):
    return pltpu.roll(x, LANES // 2, 1)


def _mod_kernel(c_ref, w_ref, b_ref, o_ref):
    c = c_ref[...]
    s = c * jax.nn.sigmoid(c)
    o_ref[0] = jnp.dot(s.astype(BF16), w_ref[0].astype(BF16), preferred_element_type=F32) + b_ref[0]


def _mods(cond, ada_w, ada_b, tn=1024):
    depth, d, n = ada_w.shape
    return pl.pallas_call(
        _mod_kernel,
        grid=(depth, n // tn),
        in_specs=[pl.BlockSpec((MOD_ROWS, d), lambda l, j: (0, 0)),
                  pl.BlockSpec((1, d, tn), lambda l, j: (l, 0, j)),
                  pl.BlockSpec((1, 1, tn), lambda l, j: (l, 0, j))],
        out_specs=pl.BlockSpec((1, MOD_ROWS, tn), lambda l, j: (l, 0, j)),
        out_shape=jax.ShapeDtypeStruct((depth, MOD_ROWS, n), F32),
        compiler_params=_params("arbitrary", "arbitrary"),
        name="adaln_mod",
    )(cond, ada_w, ada_b.reshape(depth, 1, n))


def _norm_mod_kernel(x_ref, g_ref, shift_ref, scale_ref, o_ref):
    y = _rms(x_ref[...], g_ref[...])
    o_ref[...] = (y * (1 + scale_ref[...]) + shift_ref[...]).astype(BF16)


class _Mod(NamedTuple):
    table: jax.Array
    layer: int
    k: int
    row0: int
    groups: int


def _group_spec(m, t, tm):
    tiles_per_group = t // m.groups // tm
    d = m.table.shape[-1]
    return pl.BlockSpec((None, None, None, 1, d),
                        lambda i, *_: (m.layer, m.row0 + i // tiles_per_group, m.k, 0, 0))


def _norm_mod(x, g, shift, scale, tm=1024):
    t, d = x.shape
    return pl.pallas_call(
        _norm_mod_kernel,
        grid=(t // tm,),
        in_specs=[pl.BlockSpec((tm, d), lambda i: (i, 0)),
                  pl.BlockSpec((1, d), lambda i: (0, 0)),
                  _group_spec(shift, t, tm), _group_spec(scale, t, tm)],
        out_specs=pl.BlockSpec((tm, d), lambda i: (i, 0)),
        out_shape=jax.ShapeDtypeStruct((t, d), BF16),
        compiler_params=_params("arbitrary"),
        name="norm_mod",
    )(x, g.reshape(1, d), shift.table, scale.table)


def _qkv_rope_kernel(a_ref, w_ref, cos_ref, sin_ref, o_ref, w_bf):
    @pl.when(pl.program_id(1) == 0)
    def _():
        w_bf[...] = w_ref[...].astype(BF16)

    for r in range(a_ref.shape[0] // MM_SUB_ROWS):
        rows = slice(r * MM_SUB_ROWS, (r + 1) * MM_SUB_ROWS)
        acc = jnp.dot(a_ref[rows, :], w_bf[...], preferred_element_type=F32)
        cos = cos_ref[rows, :]
        sin = sin_ref[rows, :]
        for g in range(acc.shape[1] // LANES):
            sl = slice(g * LANES, (g + 1) * LANES)
            xg = acc[:, sl]
            o_ref[rows, sl] = (xg * cos + _rot_half(xg) * sin).astype(BF16)


def _qkv_rope(a, w, layer, cos_tab, sin_tab, tm=1024, tn=1024):
    t, k = a.shape
    n = w.shape[2]
    tiles_per_kind = n // 3 // tn
    row_tiles = cos_tab.shape[1] // tm
    tab_spec = pl.BlockSpec((None, tm, LANES), lambda j, i: (j // tiles_per_kind, i % row_tiles, 0))
    return pl.pallas_call(
        _qkv_rope_kernel,
        grid=(n // tn, t // tm),
        in_specs=[pl.BlockSpec((tm, k), lambda j, i: (i, 0)),
                  pl.BlockSpec((None, k, tn), lambda j, i: (layer, 0, j)),
                  tab_spec, tab_spec],
        out_specs=pl.BlockSpec((tm, tn), lambda j, i: (i, j)),
        out_shape=jax.ShapeDtypeStruct((t, n), BF16),
        scratch_shapes=[pltpu.VMEM((k, tn), BF16)],
        compiler_params=_params("arbitrary", "arbitrary"),
        name="qkv_rope",
    )(a, w, cos_tab, sin_tab)


def _swiglu_kernel(with_cast, a_ref, wg_ref, wu_ref, *refs):
    if with_cast:
        cast_src_ref, o_ref, cast_dst_ref, wg_bf, wu_bf = refs
        cast_dst_ref[...] = cast_src_ref[...].astype(BF16)
    else:
        o_ref, wg_bf, wu_bf = refs

    @pl.when(pl.program_id(1) == 0)
    def _():
        wg_bf[...] = wg_ref[...].astype(BF16)
        wu_bf[...] = wu_ref[...].astype(BF16)

    for r in range(a_ref.shape[0] // MM_SUB_ROWS):
        rows = slice(r * MM_SUB_ROWS, (r + 1) * MM_SUB_ROWS)
        a = a_ref[rows, :]
        for c in range(wg_bf.shape[1] // MM_SUB_COLS):
            cols = slice(c * MM_SUB_COLS, (c + 1) * MM_SUB_COLS)
            g = jnp.dot(a, wg_bf[:, cols], preferred_element_type=F32)
            u = jnp.dot(a, wu_bf[:, cols], preferred_element_type=F32)
            o_ref[rows, cols] = (g * jax.nn.sigmoid(g) * u).astype(BF16)


def _swiglu_in(a, w_in, layer, tm=2048, tn=512, w_out=None):
    t, k = a.shape
    hidden = w_in.shape[2] // 2
    nj, ni = hidden // tn, t // tm
    in_specs = [pl.BlockSpec((tm, k), lambda j, i: (i, 0)),
                pl.BlockSpec((None, k, tn), lambda j, i: (layer, 0, j)),
                pl.BlockSpec((None, k, tn), lambda j, i: (layer, 0, j + nj))]
    args = [a, w_in, w_in]
    out_specs = [pl.BlockSpec((tm, tn), lambda j, i: (i, j))]
    out_shape = [jax.ShapeDtypeStruct((t, hidden), BF16)]
    if w_out is not None:
        rows, cols = w_out.shape[1:]
        chunk = rows // (nj * ni)
        assert chunk * nj * ni == rows and chunk % 16 == 0
        in_specs.append(pl.BlockSpec((None, chunk, cols), lambda j, i: (layer, j * ni + i, 0)))
        args.append(w_out)
        out_specs.append(pl.BlockSpec((chunk, cols), lambda j, i: (j * ni + i, 0)))
        out_shape.append(jax.ShapeDtypeStruct((rows, cols), BF16))
    outs = pl.pallas_call(
        functools.partial(_swiglu_kernel, w_out is not None),
        grid=(nj, ni),
        in_specs=in_specs, out_specs=out_specs, out_shape=out_shape,
        scratch_shapes=[pltpu.VMEM((k, tn), BF16), pltpu.VMEM((k, tn), BF16)],
        compiler_params=_params("arbitrary", "arbitrary"),
        name="swiglu_in",
    )(*args)
    return outs if w_out is not None else outs[0]


def _res_kernel(out_x, next_norm, a_ref, w_ref, x_ref, gate_ref, *refs):
    refs = list(refs)
    if next_norm == "mod":
        g_ref, shift_ref, scale_ref = refs[:3]
        outs = refs[3:]
    else:
        g_ref = refs[0]
        outs = refs[1:]
    ox_ref = outs.pop(0) if out_x else None
    oh_ref = outs[0]
    if next_norm == "mod":
        gain = (g_ref[...] * (1 + scale_ref[...])).astype(BF16)
        shift = shift_ref[...].astype(BF16)
    for r in range(a_ref.shape[0] // RES_SUB_ROWS):
        rows = slice(r * RES_SUB_ROWS, (r + 1) * RES_SUB_ROWS)
        y = jnp.dot(a_ref[rows, :], w_ref[...], preferred_element_type=F32)
        xn = x_ref[rows, :] + gate_ref[...] * y
        if out_x:
            ox_ref[rows, :] = xn
        if next_norm == "mod":
            ms = jnp.mean(xn * xn, axis=-1, keepdims=True)
            oh_ref[rows, :] = (xn * lax.rsqrt(ms + NORM_EPS)).astype(BF16) * gain + shift
        else:
            oh_ref[rows, :] = _rms(xn, g_ref[...])


def _res(a, w, layer, x, gate, g, shift=None, scale=None, *, out_x=True, tm):
    t, k = a.shape
    d = w.shape[2]
    next_norm = "final" if shift is None else "mod"
    row = pl.BlockSpec((tm, d), lambda i: (i, 0))
    in_specs = [pl.BlockSpec((tm, k), lambda i: (i, 0)),
                pl.BlockSpec((None, k, d), lambda i: (layer, 0, 0), pipeline_mode=pl.Buffered(1)),
                row, _group_spec(gate, t, tm), pl.BlockSpec((1, d), lambda i: (0, 0))]
    args = [a, w, x, gate.table, g.reshape(1, d)]
    if next_norm == "mod":
        in_specs += [_group_spec(shift, t, tm), _group_spec(scale, t, tm)]
        args += [shift.table, scale.table]
    out_specs, out_shape = [], []
    if out_x:
        out_specs.append(row)
        out_shape.append(jax.ShapeDtypeStruct((t, d), F32))
    out_specs.append(row)
    out_shape.append(jax.ShapeDtypeStruct((t, d), BF16 if next_norm == "mod" else F32))
    return pl.pallas_call(
        functools.partial(_res_kernel, out_x, next_norm),
        grid=(t // tm,),
        in_specs=in_specs, out_specs=out_specs, out_shape=out_shape,
        compiler_params=_params("arbitrary"),
        name="proj_residual",
    )(*args)


def _nt_dot(q, k):
    return lax.dot_general(q, k, (((1,), (1,)), ((), ())), preferred_element_type=F32)


def _softmax_pv(q, k_parts, v_refs):
    scores = [_nt_dot(q, k) for k in k_parts]
    mx = functools.reduce(jnp.maximum, [jnp.max(s, axis=1, keepdims=True) for s in scores])
    e = [jnp.exp2(s - mx) for s in scores]
    tot = functools.reduce(jnp.add, [jnp.sum(x, axis=1, keepdims=True) for x in e])
    o = functools.reduce(jnp.add, [jnp.dot(p.astype(BF16), v[...], preferred_element_type=F32)
                                   for p, v in zip(e, v_refs)])
    return o, 1.0 / tot


def _side_cast_specs(w, n_steps, step_of):
    rows, cols = w.shape[1:]
    chunk = rows // n_steps
    assert chunk * n_steps == rows and chunk % 16 == 0
    return (pl.BlockSpec((None, chunk, cols), lambda *g: (0, step_of(*g), 0)),
            pl.BlockSpec((chunk, cols), lambda *g: (step_of(*g), 0)),
            jax.ShapeDtypeStruct((rows, cols), BF16))


def _split_side_cast(with_cast, refs):
    if not with_cast:
        return refs[0]
    cast_src_ref, o_ref, cast_dst_ref = refs
    cast_dst_ref[...] = cast_src_ref[...].astype(BF16)
    return o_ref


def _da_attn_kernel(nseg, lam_init, with_cast, lamv_ref, g_ref, q_ref, *refs):
    k_refs, v_refs = refs[:nseg], refs[nseg:2 * nseg]
    o_ref = _split_side_cast(with_cast, refs[2 * nseg:])
    lv = lamv_ref[...]
    lam = (jnp.exp(jnp.sum(lv[0:1] * lv[1:2], axis=1, keepdims=True))
           - jnp.exp(jnp.sum(lv[2:3] * lv[3:4], axis=1, keepdims=True)) + lam_init)
    for hh in range(q_ref.shape[1] // DA_V_DIM):
        vcols = slice(hh * DA_V_DIM, (hh + 1) * DA_V_DIM)
        vs = [v.at[:, vcols] for v in v_refs]
        for t in range(q_ref.shape[0] // ATTN_SUB_ROWS):
            rows = slice(t * ATTN_SUB_ROWS, (t + 1) * ATTN_SUB_ROWS)
            o = None
            for m in range(2):
                sl = slice(hh * DA_V_DIM + m * DA_HEAD_DIM, hh * DA_V_DIM + (m + 1) * DA_HEAD_DIM)
                om, r = _softmax_pv(q_ref[rows, sl], [k[:, sl] for k in k_refs], vs)
                o = om * r if m == 0 else o - om * (lam * r)
            o_ref[rows, vcols] = (_rms(o, g_ref[...]) * (1.0 - lam_init)).astype(BF16)


def _da_attention(q_src, kv_srcs, lam_vecs, subln_g, lam_init, heads, tq, heads_per_step=1, w_next=None):
    b, sq, _ = q_src.shape
    hd = heads_per_step * DA_V_DIM
    groups = heads // heads_per_step
    in_specs = [pl.BlockSpec((4, DA_HEAD_DIM), lambda bi, h, qi: (0, 0)),
                pl.BlockSpec((1, DA_V_DIM), lambda bi, h, qi: (0, 0)),
                pl.BlockSpec((None, tq, hd), lambda bi, h, qi: (bi, qi, h))]
    args = [lam_vecs, subln_g.reshape(1, DA_V_DIM), q_src]
    for kind in (1, 2):
        for src in kv_srcs:
            in_specs.append(pl.BlockSpec((None, src.shape[1], hd),
                                         lambda bi, h, qi, kind=kind: (bi, 0, kind * groups + h)))
            args.append(src)
    nq = sq // tq
    out_specs = [pl.BlockSpec((None, tq, hd), lambda bi, h, qi: (bi, qi, h))]
    out_shape = [jax.ShapeDtypeStruct((b, sq, heads * DA_V_DIM), BF16)]
    if w_next is not None:
        src_spec, dst_spec, dst_shape = _side_cast_specs(
            w_next, b * groups * nq, lambda bi, h, qi: (bi * groups + h) * nq + qi)
        in_specs.append(src_spec)
        args.append(w_next)
        out_specs.append(dst_spec)
        out_shape.append(dst_shape)
    outs = pl.pallas_call(
        functools.partial(_da_attn_kernel, len(kv_srcs), lam_init, w_next is not None),
        grid=(b, groups, nq),
        in_specs=in_specs, out_specs=out_specs, out_shape=out_shape,
        compiler_params=_params("arbitrary", "arbitrary", "arbitrary"),
        name="diff_attention",
    )(*args)
    return outs if w_next is not None else outs[0]


def _mla_attn_kernel(nseg, with_cast, q_ref, *refs):
    k_refs, v_refs, aug_refs = refs[:nseg], refs[nseg:2 * nseg], refs[-nseg:]
    o_ref = _split_side_cast(with_cast, refs[2 * nseg:-nseg])

    @pl.when(pl.program_id(2) == 0)
    def _():
        for v, va in zip(v_refs, aug_refs):
            for hh in range(2):
                va[hh, :, :MLA_V] = v[:, hh * MLA_V:(hh + 1) * MLA_V]
                va[hh, :, MLA_V:] = jnp.ones((v.shape[0], MLA_V), BF16)

    for t in range(q_ref.shape[0] // ATTN_SUB_ROWS):
        rows = slice(t * ATTN_SUB_ROWS, (t + 1) * ATTN_SUB_ROWS)
        for hh in range(2):
            cols = slice(hh * MLA_QK_PAD, (hh + 1) * MLA_QK_PAD)
            q = q_ref[rows, cols]
            scores = [_nt_dot(q, k[:, cols]) for k in k_refs]
            mx = functools.reduce(jnp.maximum, [jnp.max(s, axis=1, keepdims=True) for s in scores])
            o = functools.reduce(jnp.add, [
                jnp.dot(jnp.exp2(s - mx).astype(BF16), va[hh], preferred_element_type=F32)
                for s, va in zip(scores, aug_refs)])
            half = slice(hh * MLA_V, (hh + 1) * MLA_V)
            o_ref[rows, half] = (o[:, :MLA_V] / o[:, MLA_V:]).astype(BF16)


def _mla_attention(q, ks, vs, heads, tq, w_next=None):
    b, sq, _ = q.shape
    pairs, nq = heads // 2, sq // tq
    in_specs = [pl.BlockSpec((None, tq, 2 * MLA_QK_PAD), lambda bi, h, qi: (bi, qi, h))]
    in_specs += [pl.BlockSpec((None, k.shape[1], 2 * MLA_QK_PAD), lambda bi, h, qi: (bi, 0, h)) for k in ks]
    in_specs += [pl.BlockSpec((None, v.shape[1], 2 * MLA_V), lambda bi, h, qi: (bi, 0, h)) for v in vs]
    args = [q, *ks, *vs]
    out_specs = [pl.BlockSpec((None, tq, 2 * MLA_V), lambda bi, h, qi: (bi, qi, h))]
    out_shape = [jax.ShapeDtypeStruct((b, sq, heads * MLA_V), BF16)]
    if w_next is not None:
        src_spec, dst_spec, dst_shape = _side_cast_specs(
            w_next, b * pairs * nq, lambda bi, h, qi: (bi * pairs + h) * nq + qi)
        in_specs.append(src_spec)
        args.append(w_next)
        out_specs.append(dst_spec)
        out_shape.append(dst_shape)
    outs = pl.pallas_call(
        functools.partial(_mla_attn_kernel, len(ks), w_next is not None),
        grid=(b, pairs, nq),
        in_specs=in_specs, out_specs=out_specs, out_shape=out_shape,
        scratch_shapes=[pltpu.VMEM((2, v.shape[1], 2 * MLA_V), BF16) for v in vs],
        compiler_params=_params("arbitrary", "arbitrary", "arbitrary"),
        name="mla_attention",
    )(*args)
    return outs if w_next is not None else outs[0]


def _mla_proj_kernel(with_q, a_ref, win_ref, qg_ref, kvg_ref, cos_ref, sin_ref, wk_ref, wv_ref, *refs):
    if with_q:
        wq_ref, q_ref, k_ref, v_ref = refs
    else:
        k_ref, v_ref = refs
    cos = cos_ref[...]
    sin = sin_ref[...]
    acc = jnp.dot(a_ref[...], win_ref[...], preferred_element_type=F32)
    ckv = _rms(acc[:, MLA_Q_LORA:MLA_Q_LORA + MLA_KV_LORA], kvg_ref[...]).astype(BF16)
    kr = acc[:, MLA_Q_LORA + MLA_KV_LORA:]
    kr = (kr * cos + _rot_half(kr) * sin).astype(BF16)
    v_ref[...] = jnp.dot(ckv, wv_ref[...], preferred_element_type=F32).astype(BF16)
    kn = jnp.dot(ckv, wk_ref[...], preferred_element_type=F32).astype(BF16)
    for h in range(kn.shape[1] // MLA_NOPE):
        base = h * MLA_QK_PAD
        k_ref[:, base:base + MLA_NOPE] = kn[:, h * MLA_NOPE:(h + 1) * MLA_NOPE]
        k_ref[:, base + MLA_NOPE:base + MLA_QK_PAD] = kr
    if with_q:
        cq = _rms(acc[:, :MLA_Q_LORA], qg_ref[...]).astype(BF16)
        qcos = cos * MLA_Q_SCALE
        qsin = sin * MLA_Q_SCALE
        for c in range(wq_ref.shape[1] // Q_UP_SUB_COLS):
            qa = jnp.dot(cq, wq_ref[:, c * Q_UP_SUB_COLS:(c + 1) * Q_UP_SUB_COLS], preferred_element_type=F32)
            for h in range(Q_UP_SUB_COLS // MLA_QK_PAD):
                lo = h * MLA_QK_PAD
                base = c * Q_UP_SUB_COLS + lo
                q_ref[:, base:base + MLA_NOPE] = (qa[:, lo:lo + MLA_NOPE] * MLA_Q_SCALE).astype(BF16)
                xr = qa[:, lo + MLA_NOPE:lo + MLA_QK_PAD]
                q_ref[:, base + MLA_NOPE:base + MLA_QK_PAD] = (xr * qcos + _rot_half(xr) * qsin).astype(BF16)


def _mla_proj(a, w_in, q_g, kv_g, cos_tab, sin_tab, wk, wv, wq=None, tm=512):
    t, k = a.shape
    with_q = wq is not None
    row_tiles = cos_tab.shape[0] // tm
    heads = wk.shape[1] // MLA_NOPE
    const = lambda arr: pl.BlockSpec(arr.shape, lambda i: (0, 0), pipeline_mode=pl.Buffered(1))
    tab = pl.BlockSpec((tm, LANES), lambda i: (i % row_tiles, 0))
    in_specs = [pl.BlockSpec((tm, k), lambda i: (i, 0)), const(w_in),
                pl.BlockSpec((1, MLA_Q_LORA), lambda i: (0, 0)),
                pl.BlockSpec((1, MLA_KV_LORA), lambda i: (0, 0)),
                tab, tab, const(wk), const(wv)]
    args = [a, w_in, q_g.reshape(1, -1), kv_g.reshape(1, -1), cos_tab, sin_tab, wk, wv]
    widths = [heads * MLA_QK_PAD, wv.shape[1]]
    if with_q:
        in_specs.append(const(wq))
        args.append(wq)
        widths.insert(0, wq.shape[1])
    return pl.pallas_call(
        functools.partial(_mla_proj_kernel, with_q),
        grid=(t // tm,),
        in_specs=in_specs,
        out_specs=[pl.BlockSpec((tm, w_), lambda i: (i, 0)) for w_ in widths],
        out_shape=[jax.ShapeDtypeStruct((t, w_), BF16) for w_ in widths],
        compiler_params=_params("arbitrary"),
        name="mla_proj",
    )(*args)


def _axial_angles(rows, rot_dim):
    row = jnp.repeat(jnp.arange(rows, dtype=F32), GRID_W)
    col = jnp.tile(jnp.arange(GRID_W, dtype=F32), rows)
    n_freq = rot_dim // 4
    inv_freq = ROPE_BASE ** (-jnp.arange(n_freq, dtype=F32) / n_freq)
    ang = jnp.concatenate([row[:, None] * inv_freq, col[:, None] * inv_freq], axis=-1)
    return jnp.cos(ang), jnp.sin(ang)


def _da_tables(seq, ctx_rows):
    cos, sin = _axial_angles(seq // GRID_W, DA_HEAD_DIM)
    cos = jnp.concatenate([cos, cos], axis=-1)
    sin = jnp.concatenate([-sin, sin], axis=-1)
    one, zero = jnp.ones_like(cos), jnp.zeros_like(cos)
    lat = (jnp.stack([cos * DA_Q_SCALE, cos, one]), jnp.stack([sin * DA_Q_SCALE, sin, zero]))
    one_c, zero_c = one[:ctx_rows], zero[:ctx_rows]
    ctx = (jnp.stack([one_c * DA_Q_SCALE, one_c, one_c]), jnp.stack([zero_c, zero_c, zero_c]))
    return lat, ctx


def _spread(x1, x2):
    z = jnp.zeros_like(x1)
    return jnp.concatenate([x1, z, x2, z], axis=-1)


def _mla_tables(seq, ctx_rows):
    cos, sin = _axial_angles(seq // GRID_W, MLA_ROPE)
    cos_l, sin_l = _spread(cos, cos), _spread(-sin, sin)
    cos_c, sin_c = jnp.ones_like(cos_l[:ctx_rows]), jnp.zeros_like(cos_l[:ctx_rows])
    return (cos_l, sin_l), (cos_c, sin_c)


def _mla_weight_layouts(w_in, w_q_up, w_kv_up):
    half = MLA_ROPE // 2
    lora = MLA_Q_LORA + MLA_KV_LORA
    w_in_p = jnp.concatenate([w_in[:, :lora], _spread(w_in[:, lora:lora + half], w_in[:, lora + half:])], axis=1)
    wq = w_q_up.reshape(MLA_Q_LORA, -1, MLA_NOPE + MLA_ROPE)
    wq_p = jnp.concatenate([wq[..., :MLA_NOPE],
                            _spread(wq[..., MLA_NOPE:MLA_NOPE + half], wq[..., MLA_NOPE + half:])], axis=-1)
    wkv = w_kv_up.reshape(MLA_KV_LORA, -1, MLA_NOPE + MLA_V)
    wk = wkv[..., :MLA_NOPE].reshape(MLA_KV_LORA, -1)
    wv = wkv[..., MLA_NOPE:].reshape(MLA_KV_LORA, -1)
    return w_in_p, wq_p.reshape(MLA_Q_LORA, -1), wk, wv


def kernel(x, c, ctx, c_ctx, ada_w, ada_b, norm_mix_g, norm_ffn_g, ffn_w_in, ffn_w_out, da_w_qkv, da_lambda, da_subln_g, da_w_o, mla_w_in, mla_q_norm_g, mla_w_q_up, mla_kv_norm_g, mla_w_kv_up, mla_w_o, final_norm_g):
    b, s, d = x.shape
    cl = ctx.shape[1]
    assert ada_w.shape[0] == 2 and b + 1 <= MOD_ROWS
    da_heads = d // DA_V_DIM
    mla_heads = d // MLA_V

    cond = jnp.concatenate([c, c_ctx[None], jnp.zeros((MOD_ROWS - b - 1, d), F32)], axis=0)
    mods = _mods(cond, ada_w, ada_b).reshape(ada_w.shape[0], MOD_ROWS, 6, 1, d)

    def lat_mod(i, k):
        return _Mod(mods, i, k, 0, b)

    def ctx_mod(i, k):
        return _Mod(mods, i, k, b, 1)

    bf = lambda w: w.astype(BF16)
    w_in_p, wq_p, wk, wv = _mla_weight_layouts(bf(mla_w_in[0]), bf(mla_w_q_up[0]), bf(mla_w_kv_up[0]))
    da_lat_tab, da_ctx_tab = _da_tables(s, 1024)
    mla_lat_tab, mla_ctx_tab = _mla_tables(s, 512)

    xl = x.reshape(b * s, d)
    xc = ctx.reshape(b * cl, d)

    lam_init = 0.8 - 0.6 * math.exp(-0.3 * 0)
    h_l = _norm_mod(xl, norm_mix_g[0], lat_mod(0, 0), lat_mod(0, 1))
    h_c = _norm_mod(xc, norm_mix_g[0], ctx_mod(0, 0), ctx_mod(0, 1))
    qkv_l = _qkv_rope(h_l, da_w_qkv, 0, *da_lat_tab).reshape(b, s, -1)
    qkv_c = _qkv_rope(h_c, da_w_qkv, 0, *da_ctx_tab).reshape(b, cl, -1)
    a_l, w_o0 = _da_attention(qkv_l, [qkv_c, qkv_l], da_lambda[0], da_subln_g[0], lam_init, da_heads, tq=s,
                              w_next=da_w_o)
    a_c = _da_attention(qkv_c, [qkv_c], da_lambda[0], da_subln_g[0], lam_init, da_heads, tq=cl,
                        heads_per_step=da_heads)
    xl, hf_l = _res(a_l.reshape(b * s, d), w_o0[None], 0, xl, lat_mod(0, 2), norm_ffn_g[0], lat_mod(0, 3), lat_mod(0, 4), tm=512)
    xc, hf_c = _res(a_c.reshape(b * cl, d), w_o0[None], 0, xc, ctx_mod(0, 2), norm_ffn_g[0], ctx_mod(0, 3), ctx_mod(0, 4), tm=512)
    act_l, w_out0 = _swiglu_in(hf_l, ffn_w_in, 0, w_out=ffn_w_out)
    act_c = _swiglu_in(hf_c, ffn_w_in, 0)
    xl, hm_l = _res(act_l, w_out0[None], 0, xl, lat_mod(0, 5), norm_mix_g[1], lat_mod(1, 0), lat_mod(1, 1), tm=512)
    (hm_c,) = _res(act_c, w_out0[None], 0, xc, ctx_mod(0, 5), norm_mix_g[1], ctx_mod(1, 0), ctx_mod(1, 1),
                   out_x=False, tm=512)

    q_l, k_l, v_l = _mla_proj(hm_l, w_in_p, mla_q_norm_g[0], mla_kv_norm_g[0], *mla_lat_tab, wk, wv, wq_p)
    k_c, v_c = _mla_proj(hm_c, w_in_p, mla_q_norm_g[0], mla_kv_norm_g[0], *mla_ctx_tab, wk, wv)
    a_l, w_o1 = _mla_attention(q_l.reshape(b, s, -1),
                               [k_c.reshape(b, cl, -1), k_l.reshape(b, s, -1)],
                               [v_c.reshape(b, cl, -1), v_l.reshape(b, s, -1)], mla_heads, tq=s, w_next=mla_w_o)
    xl, hf_l = _res(a_l.reshape(b * s, d), w_o1[None], 0, xl, lat_mod(1, 2), norm_ffn_g[1], lat_mod(1, 3), lat_mod(1, 4), tm=512)
    act_l, w_out1 = _swiglu_in(hf_l, ffn_w_in, 1, w_out=ffn_w_out)
    (out,) = _res(act_l, w_out1[None], 0, xl, lat_mod(1, 5), final_norm_g, out_x=False, tm=512)
    return out.reshape(b, s, d)
```
